```python
import math
import jax, jax.numpy as jnp
from jax import lax
import numpy as np

D_MODEL = 1024
BATCH = 16
SEQ = 2048
DEPTH = 1
DEC_BATCH = 32
DEC_SEQ = 4
PAST_LEN = 16384
PAGE_SIZE = 128

N_META = 16
GLA_HEADS = 4
GLA_DK = D_MODEL // 16
GLA_DV = D_MODEL // 8
GLA_RANK = 16
GLA_TAU = 16.0
GLA_CHUNK = 64
DIFF_HEADS = 4
DIFF_DQK = D_MODEL // 16
DIFF_DV = 2 * DIFF_DQK
Q_BLOCK = 128
GLA_QK_W = GLA_HEADS * GLA_DK
GLA_WIDTH = GLA_HEADS * GLA_DV
DIFF_QK_W = DIFF_HEADS * 2 * DIFF_DQK
DIFF_WIDTH = DIFF_HEADS * DIFF_DV
MIX_WIDTH = GLA_WIDTH + DIFF_WIDTH
IN_SPLITS = (GLA_QK_W, GLA_QK_W, GLA_WIDTH, GLA_WIDTH, GLA_RANK, DIFF_QK_W, DIFF_QK_W, DIFF_WIDTH)
IN_WIDTH = sum(IN_SPLITS)
N_GROUPS = 4
EXPERTS_PER_GROUP = 8
N_EXPERTS = N_GROUPS * EXPERTS_PER_GROUP
TOP_K = 2
D_EXPERT = D_MODEL // 2
MOE_BLOCK = 128
EPS = 1e-6

kernel_name = "hymba_gla_diffattn_hmoe_step"


def rms_norm(x, gain):
    xf = x.astype(jnp.float32)
    y = xf * lax.rsqrt(jnp.mean(xf * xf, axis=-1, keepdims=True) + EPS)
    return (y * gain.astype(jnp.float32)).astype(x.dtype)


def split_cols(z):
    offs, acc = [], 0
    for w in IN_SPLITS[:-1]:
        acc += w
        offs.append(acc)
    return jnp.split(z, offs, axis=-1)


def mixer_inputs(xn, w_in, w_gk2, b_gk2, q_gain, k_gain):
    B, T, _ = xn.shape
    gq, gk, gv, gg, glr, dq, dk, dv = split_cols(xn @ w_in)
    gq = gq.reshape(B, T, GLA_HEADS, GLA_DK) * (GLA_DK ** -0.5)
    gk = gk.reshape(B, T, GLA_HEADS, GLA_DK)
    gv = gv.reshape(B, T, GLA_HEADS, GLA_DV)
    glog = jax.nn.log_sigmoid((glr @ w_gk2 + b_gk2).astype(jnp.float32)) / GLA_TAU
    glog = glog.reshape(B, T, GLA_HEADS, GLA_DK)
    dq = rms_norm(dq.reshape(B, T, DIFF_HEADS, 2, DIFF_DQK), q_gain)
    dk = rms_norm(dk.reshape(B, T, DIFF_HEADS, 2, DIFF_DQK), k_gain).reshape(B, T, DIFF_HEADS, 2 * DIFF_DQK)
    dv = dv.reshape(B, T, DIFF_HEADS, DIFF_DV)
    return (gq, gk, gv, glog, gg), (dq, dk, dv)


def gla_chunked(q, k, v, glog, s0, chunk):
    B, T, H, _ = q.shape
    nc = T // chunk

    def to_chunks(a):
        return jnp.moveaxis(a.astype(jnp.float32).reshape(B, nc, chunk, H, a.shape[-1]), 1, 0)

    causal = jnp.tril(jnp.ones((chunk, chunk), bool))[None, :, :, None, None]

    def step(s, inp):
        qc, kc, vc, gc = inp
        b = jnp.cumsum(gc, axis=1)
        o_inter = jnp.einsum('bchk,bhkv->bchv', qc * jnp.exp(b), s)
        rel = b[:, :, None] - b[:, None, :]
        decay = jnp.exp(jnp.where(causal, rel, -jnp.inf))
        a = jnp.einsum('bihk,bjhk,bijhk->bhij', qc, kc, decay)
        o_intra = jnp.einsum('bhij,bjhv->bihv', a, vc)
        b_last = b[:, -1]
        k_dec = kc * jnp.exp(b_last[:, None] - b)
        s_new = jnp.exp(b_last)[..., None] * s + jnp.einsum('bchk,bchv->bhkv', k_dec, vc)
        return s_new, o_inter + o_intra

    s_fin, o = lax.scan(step, s0.astype(jnp.float32), (to_chunks(q), to_chunks(k), to_chunks(v), to_chunks(glog)))
    return jnp.moveaxis(o, 0, 1).reshape(B, T, H, v.shape[-1]), s_fin


def diff_scores(q, k):
    B, Tk, H, _ = k.shape
    kk = k.reshape(B, Tk, H, 2, DIFF_DQK)
    return jnp.einsum('bqhcd,bkhcd->bhcqk', q, kk).astype(jnp.float32) * (DIFF_DQK ** -0.5)


def diff_probs(s, mask, lam):
    p = jax.nn.softmax(jnp.where(mask, s, -jnp.inf), axis=-1)
    return p[:, :, 0] - lam * p[:, :, 1]


def diff_attn_prompt(q, k, v, lam):
    B, T = q.shape[:2]
    n_blk = -(-T // Q_BLOCK)
    tp = n_blk * Q_BLOCK
    qp = jnp.pad(q, ((0, 0), (0, tp - T), (0, 0), (0, 0), (0, 0)))
    qb = jnp.moveaxis(qp.reshape(B, n_blk, Q_BLOCK, DIFF_HEADS, 2, DIFF_DQK), 1, 0)
    kpos = jnp.arange(T)

    def block(args):
        qblk, i = args
        qpos = i * Q_BLOCK + jnp.arange(Q_BLOCK)
        w = diff_probs(diff_scores(qblk, k), qpos[:, None] >= kpos[None, :], lam)
        return jnp.einsum('bhqk,bkhv->bqhv', w, v.astype(jnp.float32))

    o = lax.map(block, (qb, jnp.arange(n_blk)))
    return jnp.moveaxis(o, 0, 1).reshape(B, tp, DIFF_HEADS, DIFF_DV)[:, :T]


def diff_attn_sample(q, k_new, v_new, k_past, v_past, lam):
    tn, p = q.shape[1], k_past.shape[1]
    s = jnp.concatenate([diff_scores(q, k_past), diff_scores(q, k_new)], axis=-1)
    mask = jnp.concatenate([jnp.ones((tn, p), bool), jnp.tril(jnp.ones((tn, tn), bool))], axis=1)
    w = diff_probs(s, mask, lam)
    return (jnp.einsum('bhqk,bkhv->bqhv', w[..., :p], v_past.astype(jnp.float32))
            + jnp.einsum('bhqk,bkhv->bqhv', w[..., p:], v_new.astype(jnp.float32)))


def merge_heads(o_gla, gg, o_diff, gla_gain, diff_gain, lam_init, w_out, dt):
    B, T = o_gla.shape[:2]
    a = rms_norm(o_gla, gla_gain) * jax.nn.silu(gg.reshape(B, T, GLA_HEADS, GLA_DV).astype(jnp.float32))
    d = rms_norm(o_diff, diff_gain) * (1.0 - lam_init)
    cat = jnp.concatenate([a.reshape(B, T, GLA_WIDTH), d.reshape(B, T, DIFF_WIDTH)], axis=-1).astype(dt)
    return cat @ w_out


def grouped_experts(xf, eidx, gates, w1, w3, w2):
    n, d = xf.shape
    m = n * TOP_K
    e = eidx.reshape(m)
    tok = jnp.repeat(jnp.arange(n, dtype=jnp.int32), TOP_K)
    gate = gates.reshape(m)
    order = jnp.argsort(e)
    e_s, tok_s, gate_s = e[order], tok[order], gate[order]
    counts = jnp.bincount(e, length=N_EXPERTS)
    padded = (counts + MOE_BLOCK - 1) // MOE_BLOCK * MOE_BLOCK
    pad_end = jnp.cumsum(padded)
    pad_start = pad_end - padded
    start = jnp.cumsum(counts) - counts
    dest = pad_start[e_s] + jnp.arange(m) - start[e_s]
    n_blk = -(-(m + N_EXPERTS * (MOE_BLOCK - 1)) // MOE_BLOCK)
    cap = n_blk * MOE_BLOCK
    slot_tok = jnp.full((cap,), n, jnp.int32).at[dest].set(tok_s)
    blk_expert = jnp.minimum(jnp.searchsorted(pad_end, jnp.arange(n_blk) * MOE_BLOCK, side='right'), N_EXPERTS - 1)
    x_pad = jnp.concatenate([xf, jnp.zeros((1, d), xf.dtype)], axis=0)
    xb = x_pad[slot_tok].reshape(n_blk, MOE_BLOCK, d)

    def run(args):
        xblk, ei = args
        h = jax.nn.silu(xblk @ w1[ei]) * (xblk @ w3[ei])
        return h @ w2[ei]

    yb = lax.map(run, (xb, blk_expert)).reshape(cap, d)
    contrib = yb[dest] * gate_s[:, None].astype(yb.dtype)
    return jnp.zeros((n, d), yb.dtype).at[tok_s].add(contrib)


def hier_moe(x, w_rg, b_rg, w_re, b_re, w1, w3, w2):
    B, T, D = x.shape
    n = B * T
    xf = x.reshape(n, D)
    lg = (xf @ w_rg + b_rg).astype(jnp.float32)
    g = jnp.argmax(lg, axis=-1)
    g_gate = jnp.take_along_axis(jax.nn.softmax(lg, axis=-1), g[:, None], axis=-1)
    le = (xf @ w_re + b_re).astype(jnp.float32).reshape(n, N_GROUPS, EXPERTS_PER_GROUP)
    le_g = jnp.take_along_axis(le, g[:, None, None], axis=1)[:, 0]
    top_v, top_i = lax.top_k(le_g, TOP_K)
    gates = g_gate * jax.nn.softmax(top_v, axis=-1)
    eidx = g[:, None].astype(jnp.int32) * EXPERTS_PER_GROUP + top_i.astype(jnp.int32)
    return grouped_experts(xf, eidx, gates, w1, w3, w2).reshape(B, T, D).astype(x.dtype)


def setup_inputs(seed: int = 0) -> dict:
    key = jax.random.key(seed)
    ks = jax.random.split(key, 32)
    n_pages = PAST_LEN // PAGE_SIZE
    n_used = DEC_BATCH * n_pages
    n_phys = (5 * n_used + 3) // 4
    L = DEPTH

    def nrm(k, shape, scale):
        return jax.random.normal(k, shape, jnp.float32) * scale

    page_table = jax.random.permutation(ks[0], n_phys)[:n_used].reshape(DEC_BATCH, n_pages).astype(jnp.int32)
    return {
        "x_prompt": nrm(ks[1], (BATCH, SEQ, D_MODEL), 1.0),
        "x_sample": nrm(ks[2], (DEC_BATCH, DEC_SEQ, D_MODEL), 1.0),
        "cache_k": nrm(ks[3], (n_phys, PAGE_SIZE, L, DIFF_HEADS, 2 * DIFF_DQK), 1.0),
        "cache_v": nrm(ks[4], (n_phys, PAGE_SIZE, L, DIFF_HEADS, DIFF_DV), 1.0),
        "state_gla": nrm(ks[5], (DEC_BATCH, L, GLA_HEADS, GLA_DK, GLA_DV), 0.3),
        "page_table": page_table,
        "meta_tokens": nrm(ks[6], (N_META, D_MODEL), 1.0),
        "norm1": 1.0 + nrm(ks[7], (L, D_MODEL), 0.02),
        "w_in": nrm(ks[8], (L, D_MODEL, IN_WIDTH), D_MODEL ** -0.5),
        "w_gk2": nrm(ks[9], (L, GLA_RANK, GLA_QK_W), GLA_RANK ** -0.5),
        "b_gk2": nrm(ks[10], (L, GLA_QK_W), 0.1),
        "gla_norm": 1.0 + nrm(ks[11], (L, GLA_DV), 0.02),
        "q_norm": 1.0 + nrm(ks[12], (L, 2, DIFF_DQK), 0.02),
        "k_norm": 1.0 + nrm(ks[13], (L, 2, DIFF_DQK), 0.02),
        "lam_q1": nrm(ks[14], (L, DIFF_DQK), 0.1),
        "lam_k1": nrm(ks[15], (L, DIFF_DQK), 0.1),
        "lam_q2": nrm(ks[16], (L, DIFF_DQK), 0.1),
        "lam_k2": nrm(ks[17], (L, DIFF_DQK), 0.1),
        "diff_norm": 1.0 + nrm(ks[18], (L, DIFF_DV), 0.02),
        "w_out": nrm(ks[19], (L, MIX_WIDTH, D_MODEL), MIX_WIDTH ** -0.5),
        "norm2": 1.0 + nrm(ks[20], (L, D_MODEL), 0.02),
        "w_route_group": nrm(ks[21], (L, D_MODEL, N_GROUPS), D_MODEL ** -0.5),
        "b_route_group": nrm(ks[22], (L, N_GROUPS), 0.01),
        "w_route_expert": nrm(ks[23], (L, D_MODEL, N_EXPERTS), D_MODEL ** -0.5),
        "b_route_expert": nrm(ks[24], (L, N_EXPERTS), 0.01),
        "w_up": nrm(ks[25], (L, N_EXPERTS, D_MODEL, D_EXPERT), D_MODEL ** -0.5),
        "w_gate": nrm(ks[26], (L, N_EXPERTS, D_MODEL, D_EXPERT), D_MODEL ** -0.5),
        "w_down": nrm(ks[27], (L, N_EXPERTS, D_EXPERT, D_MODEL), D_EXPERT ** -0.5),
    }


def reference(x_prompt, x_sample, cache_k, cache_v, state_gla, page_table,
              meta_tokens, norm1, w_in, w_gk2, b_gk2, gla_norm, q_norm, k_norm,
              lam_q1, lam_k1, lam_q2, lam_k2, diff_norm, w_out, norm2,
              w_route_group, b_route_group, w_route_expert, b_route_expert,
              w_up, w_gate, w_down):
    B = x_prompt.shape[0]
    DB = x_sample.shape[0]
    dt = x_prompt.dtype
    hp = jnp.concatenate([jnp.broadcast_to(meta_tokens.astype(dt)[None], (B, N_META, D_MODEL)), x_prompt], axis=1)
    hs = x_sample
    pad = GLA_CHUNK - N_META
    past = page_table.shape[1] * PAGE_SIZE
    kp_l, vp_l, sp_l, ks_l, vs_l, ss_l = [], [], [], [], [], []
    for l in range(DEPTH):
        lam_init = 0.8 - 0.6 * math.exp(-0.3 * l)
        lam = (jnp.exp(jnp.sum(lam_q1[l].astype(jnp.float32) * lam_k1[l].astype(jnp.float32)))
               - jnp.exp(jnp.sum(lam_q2[l].astype(jnp.float32) * lam_k2[l].astype(jnp.float32))) + lam_init)

        xn = rms_norm(hp, norm1[l])
        (gq, gk, gv, glog, gg), (dq, dk, dv) = mixer_inputs(xn, w_in[l], w_gk2[l], b_gk2[l], q_norm[l], k_norm[l])

        def padf(a):
            return jnp.pad(a, ((0, 0), (pad, 0), (0, 0), (0, 0)))

        s0 = jnp.zeros((B, GLA_HEADS, GLA_DK, GLA_DV), jnp.float32)
        o_gla, s_fin = gla_chunked(padf(gq), padf(gk), padf(gv), padf(glog), s0, GLA_CHUNK)
        o_gla = o_gla[:, pad:]
        o_diff = diff_attn_prompt(dq, dk, dv, lam)
        hp = hp + merge_heads(o_gla, gg, o_diff, gla_norm[l], diff_norm[l], lam_init, w_out[l], dt)
        kp_l.append(dk)
        vp_l.append(dv)
        sp_l.append(s_fin)

        xn_s = rms_norm(hs, norm1[l])
        (gq, gk, gv, glog, gg), (dq, dk, dv) = mixer_inputs(xn_s, w_in[l], w_gk2[l], b_gk2[l], q_norm[l], k_norm[l])
        o_gla_s, s_new = gla_chunked(gq, gk, gv, glog, state_gla[:, l], hs.shape[1])
        k_past = cache_k[page_table, :, l].reshape(DB, past, DIFF_HEADS, 2 * DIFF_DQK)
        v_past = cache_v[page_table, :, l].reshape(DB, past, DIFF_HEADS, DIFF_DV)
        o_diff_s = diff_attn_sample(dq, dk, dv, k_past, v_past, lam)
        hs = hs + merge_heads(o_gla_s, gg, o_diff_s, gla_norm[l], diff_norm[l], lam_init, w_out[l], dt)
        ks_l.append(dk)
        vs_l.append(dv)
        ss_l.append(s_new)

        if l == DEPTH - 1:
            hp = hp[:, N_META:]
        hp = hp + hier_moe(rms_norm(hp, norm2[l]), w_route_group[l], b_route_group[l], w_route_expert[l],
                           b_route_expert[l], w_up[l], w_gate[l], w_down[l])
        hs = hs + hier_moe(rms_norm(hs, norm2[l]), w_route_group[l], b_route_group[l], w_route_expert[l],
                           b_route_expert[l], w_up[l], w_gate[l], w_down[l])

    y_prompt = hp
    y_sample = hs
    k_prompt = jnp.stack(kp_l, axis=2)
    v_prompt = jnp.stack(vp_l, axis=2)
    gla_prompt = jnp.stack(sp_l, axis=1)
    k_sample = jnp.stack(ks_l, axis=2)
    v_sample = jnp.stack(vs_l, axis=2)
    gla_sample = jnp.stack(ss_l, axis=1)
    return (y_prompt, y_sample, k_prompt, v_prompt, gla_prompt, k_sample, v_sample, gla_sample)
```

```python
import functools

import jax
import jax.numpy as jnp
from jax import lax
from jax.experimental import pallas as pl
from jax.experimental.pallas import tpu as pltpu

F32 = jnp.float32
BF16 = jnp.bfloat16
I32 = jnp.int32

V7X_LANES = 128
V7X_SUBLANES = 8
V7X_BF16_SUBLANES = 16
V7X_VMEM_LIMIT_BYTES = 56 * 1024 * 1024

D_MODEL = 1024
N_META = 16
N_HEADS = 4
GLA_DK = 64
GLA_DV = 128
GLA_RANK = 16
GLA_TAU = 16.0
GLA_QK_W = N_HEADS * GLA_DK
GLA_WIDTH = N_HEADS * GLA_DV
DIFF_DQK = 64
DIFF_DV = 128
DIFF_W = N_HEADS * DIFF_DV
N_GROUPS = 4
EXPERTS_PER_GROUP = 8
N_EXPERTS = N_GROUPS * EXPERTS_PER_GROUP
D_EXPERT = D_MODEL // 2
EPS = 1e-6
LAM_INIT = 0.2
NEG_BIG = -1e30
EXP_CLAMP = 80.0

OFF_GQ, OFF_GK, OFF_GV, OFF_GG = 0, 256, 512, 1024
OFF_DQ, OFF_DK, OFF_DV, OFF_GLR = 1536, 2048, 2560, 3072
PACKED_IN_W = OFF_GLR + V7X_LANES

GLA_CHUNK = 64
ATTN_TQ = 256
MOE_BLOCK = 256
PAGES_PER_STEP = 8


def _cparams(sem):
    return pltpu.CompilerParams(dimension_semantics=sem, vmem_limit_bytes=V7X_VMEM_LIMIT_BYTES)


def _row_tile(n, pref):
    t = min(n, pref)
    while n % t:
        t //= 2
    return t


def _dot(a, b):
    return jnp.dot(a, b, preferred_element_type=F32)


def _dot_nt(a, b):
    return lax.dot_general(a, b, (((1,), (1,)), ((), ())), preferred_element_type=F32)


def _dot_tn(a, b):
    return lax.dot_general(a, b, (((0,), (0,)), ((), ())), preferred_element_type=F32)


def _iota_div(shape, dim, d):
    assert d & (d - 1) == 0
    return lax.shift_right_logical(lax.broadcasted_iota(I32, shape, dim), d.bit_length() - 1)


def _iota_mod(shape, dim, d):
    assert d & (d - 1) == 0
    return lax.broadcasted_iota(I32, shape, dim) & (d - 1)


def _split_bf16(x):
    hi = x.astype(BF16)
    lo = (x - hi.astype(F32)).astype(BF16)
    return hi, lo


def _inproj_body(x_ref, n1_ref, w_ref, wg2_ref, bg2_ref, qg_ref, kg_ref, gm_ref,
                 gq_ref, gk_ref, gv_ref, gg_ref, glog_ref, dq_ref, dk_ref, dv_ref, dkb_ref, dvb_ref):
    x = x_ref[...]
    ms = jnp.mean(x * x, axis=-1, keepdims=True)
    xn = (x * lax.rsqrt(ms + EPS) * n1_ref[...]).astype(BF16)

    def proj(off, width):
        return _dot(xn, w_ref[:, off:off + width])

    gq_ref[...] = proj(OFF_GQ, GLA_QK_W) * (GLA_DK ** -0.5)
    gk_ref[...] = proj(OFF_GK, GLA_QK_W)
    gv_ref[...] = proj(OFF_GV, GLA_WIDTH)
    gg_ref[...] = proj(OFF_GG, GLA_WIDTH)

    glr = proj(OFF_GLR, V7X_LANES).astype(BF16)
    gpre = _dot(glr, wg2_ref[...]) + bg2_ref[...]
    log_sig = jnp.minimum(gpre, 0.0) - jnp.log(1.0 + jnp.exp(-jnp.abs(gpre)))
    glog_ref[...] = log_sig * (1.0 / GLA_TAU)

    gm = gm_ref[...]

    def group_norm(z, gain):
        hi, lo = _split_bf16(z * z)
        parts = []
        for c in range(DIFF_W // 256):
            sl = slice(256 * c, 256 * (c + 1))
            parts.append(_dot(hi[:, sl], gm) + _dot(lo[:, sl], gm))
        ss = jnp.concatenate(parts, axis=-1)
        return z * lax.rsqrt(ss * (1.0 / DIFF_DQK) + EPS) * gain

    dq = group_norm(proj(OFF_DQ, DIFF_W), qg_ref[...])
    dq_ref[...] = (dq * (DIFF_DQK ** -0.5)).astype(BF16)
    dk = group_norm(proj(OFF_DK, DIFF_W), kg_ref[...])
    dk_ref[...] = dk
    dkb_ref[...] = dk.astype(BF16)
    dv = proj(OFF_DV, DIFF_W)
    dv_ref[...] = dv
    dvb_ref[...] = dv.astype(BF16)


def _inproj(x2d, prep, tile):
    n = x2d.shape[0]
    tm = _row_tile(n, tile)
    row = lambda w: pl.BlockSpec((tm, w), lambda i: (i, 0))
    full = lambda a: pl.BlockSpec(a.shape, lambda i: (0,) * a.ndim)
    consts = (prep["norm1"], prep["w_in"], prep["w_gk2"], prep["b_gk2"], prep["q_gain"], prep["k_gain"],
              prep["group_ones"])
    widths = (GLA_QK_W, GLA_QK_W, GLA_WIDTH, GLA_WIDTH, GLA_QK_W, DIFF_W, DIFF_W, DIFF_W, DIFF_W, DIFF_W)
    dtypes = (F32, F32, F32, F32, F32, BF16, F32, F32, BF16, BF16)
    return pl.pallas_call(
        _inproj_body,
        grid=(n // tm,),
        in_specs=[row(D_MODEL)] + [full(c) for c in consts],
        out_specs=[row(w) for w in widths],
        out_shape=[jax.ShapeDtypeStruct((n, w), d) for w, d in zip(widths, dtypes)],
        compiler_params=_cparams(("arbitrary",)),
        name="inproj",
    )(x2d, *consts)


def _gla_body(q_ref, k_ref, g_ref, v_ref, s0_ref, o_ref, sfin_ref, state_ref, *, chunk, n_sub, valid):
    t = pl.program_id(1)
    c = chunk

    @pl.when(t == 0)
    def _():
        state_ref[...] = jnp.zeros_like(state_ref)
        for h in range(N_HEADS):
            state_ref[GLA_DK * h:GLA_DK * (h + 1), GLA_DV * h:GLA_DV * (h + 1)] = s0_ref[0, h]

    ri = lax.broadcasted_iota(I32, (c, c), 0)
    ci = lax.broadcasted_iota(I32, (c, c), 1)
    tri = (ci <= ri).astype(BF16)
    ones_cols = jnp.ones((c, V7X_LANES), BF16)
    k_shape = (N_HEADS * c, GLA_QK_W)
    k_head_mask = _iota_div(k_shape, 0, c) == _iota_div(k_shape, 1, GLA_DK)
    v_shape = (N_HEADS * c, GLA_WIDTH)
    v_head_mask = _iota_div(v_shape, 0, c) == _iota_div(v_shape, 1, GLA_DV)
    s_shape = (GLA_QK_W, GLA_WIDTH)
    s_head_mask = _iota_div(s_shape, 0, GLA_DK) == _iota_div(s_shape, 1, GLA_DV)
    a_shape = (c, N_HEADS * c)
    causal = _iota_mod(a_shape, 1, c) <= lax.broadcasted_iota(I32, a_shape, 0)
    row_id = lax.broadcasted_iota(I32, (c, GLA_QK_W), 0)
    mid = c // 2 - 1

    for sub in range(n_sub):
        rows = slice(sub * c, (sub + 1) * c)
        q = q_ref[0, rows, :]
        k = k_ref[0, rows, :]
        g = g_ref[0, rows, :]
        v = v_ref[0, rows, :]
        if valid is not None:
            g = jnp.where(row_id < valid, g, 0.0)
        g_hi, g_lo = _split_bf16(g)
        b = _dot(tri, g_hi) + _dot(tri, g_lo)
        b_last_col = _dot_tn(g_hi, ones_cols) + _dot_tn(g_lo, ones_cols)
        b_last = b[c - 1:c, :]
        b_mid = b[mid:mid + 1, :]

        state = state_ref[...]
        q_dec = (q * jnp.exp(b)).astype(BF16)
        o_inter = _dot(q_dec, state.astype(BF16))

        q_t = (q * jnp.exp(jnp.minimum(b - b_mid, EXP_CLAMP))).astype(BF16)
        k_t = (k * jnp.exp(jnp.minimum(b_mid - b, EXP_CLAMP))).astype(BF16)
        k_rows = jnp.where(k_head_mask, jnp.concatenate([k_t] * N_HEADS, axis=0), 0)
        a = _dot_nt(q_t, k_rows)
        a = jnp.where(causal, a, 0.0).astype(BF16)
        v_bf = v.astype(BF16)
        v_rows = jnp.where(v_head_mask, jnp.concatenate([v_bf] * N_HEADS, axis=0), 0)
        o_ref[0, rows, :] = o_inter + _dot(a, v_rows)

        k_dec = (k * jnp.exp(b_last - b)).astype(BF16)
        ds = _dot_tn(k_dec, v_bf)
        decay = jnp.exp(jnp.concatenate([b_last_col] * (GLA_WIDTH // V7X_LANES), axis=1))
        state_ref[...] = decay * state + jnp.where(s_head_mask, ds, 0.0)

    @pl.when(t == pl.num_programs(1) - 1)
    def _():
        for h in range(N_HEADS):
            sfin_ref[0, h] = state_ref[GLA_DK * h:GLA_DK * (h + 1), GLA_DV * h:GLA_DV * (h + 1)]


def _gla(gq, gk, glog, gv, s0, *, chunk, n_sub, valid=None):
    bsz, tlen, _ = gq.shape
    tb = chunk * n_sub
    assert tlen % tb == 0
    s0_map = (lambda b, t: (b, 0, 0, 0)) if s0.shape[0] == bsz else (lambda b, t: (0, 0, 0, 0))
    seq = lambda w: pl.BlockSpec((1, tb, w), lambda b, t: (b, t, 0))
    st = (1, N_HEADS, GLA_DK, GLA_DV)
    return pl.pallas_call(
        functools.partial(_gla_body, chunk=chunk, n_sub=n_sub, valid=valid),
        grid=(bsz, tlen // tb),
        in_specs=[seq(GLA_QK_W), seq(GLA_QK_W), seq(GLA_QK_W), seq(GLA_WIDTH), pl.BlockSpec(st, s0_map)],
        out_specs=[seq(GLA_WIDTH), pl.BlockSpec(st, lambda b, t: (b, 0, 0, 0))],
        out_shape=[jax.ShapeDtypeStruct((bsz, tlen, GLA_WIDTH), F32),
                   jax.ShapeDtypeStruct((bsz, N_HEADS, GLA_DK, GLA_DV), F32)],
        scratch_shapes=[pltpu.VMEM((GLA_QK_W, GLA_WIDTH), F32)],
        compiler_params=_cparams(("arbitrary", "arbitrary")),
        name="gla_scan",
    )(gq, gk, glog, gv, s0)


def _attn_body(lam_ref, q_ref, k_ref, v_ref, o_ref, *, tq, n_meta):
    i = pl.program_id(2)
    q = q_ref[0]
    lane = lax.broadcasted_iota(I32, (tq, DIFF_DV), 1)
    zero = jnp.zeros_like(q)
    qs = jnp.concatenate([jnp.where(lane < DIFF_DQK, q, zero), jnp.where(lane >= DIFF_DQK, q, zero)], axis=0)

    def step(start, size, carry, mask):
        m, l, acc = carry
        k = k_ref[0, pl.ds(start, size), :]
        v = v_ref[0, pl.ds(start, size), :]
        s = _dot_nt(qs, k)
        if mask is not None:
            s = jnp.where(mask, s, NEG_BIG)
        m_new = jnp.maximum(m, jnp.max(s, axis=-1, keepdims=True))
        alpha = jnp.exp(m - m_new)
        p = jnp.exp(s - m_new)
        l = alpha * l + jnp.sum(p, axis=-1, keepdims=True)
        acc = alpha * acc + _dot(p.astype(BF16), v)
        return m_new, l, acc

    carry = (jnp.full((2 * tq, 1), NEG_BIG, F32), jnp.zeros((2 * tq, 1), F32), jnp.zeros((2 * tq, DIFF_DV), F32))
    carry = step(0, n_meta, carry, None)

    def full_chunk(j, c):
        return step(pl.multiple_of(n_meta + j * tq, 16), tq, c, None)

    carry = lax.fori_loop(0, i, full_chunk, carry)
    r = _iota_mod((2 * tq, tq), 0, tq)
    cidx = lax.broadcasted_iota(I32, (2 * tq, tq), 1)
    m, l, acc = step(pl.multiple_of(n_meta + i * tq, 16), tq, carry, cidx <= r)
    o = acc / l
    o_ref[0] = o[:tq] - lam_ref[0] * o[tq:]


def _attn_prompt(lam, dq, kb, vb, *, tq):
    bsz, tlen, _ = dq.shape
    tk = kb.shape[1]
    assert tlen % tq == 0 and tk == tlen + N_META
    kv_spec = pl.BlockSpec((1, tk, DIFF_DV), lambda b, h, i: (b, 0, h))
    q_spec = pl.BlockSpec((1, tq, DIFF_DV), lambda b, h, i: (b, i, h))
    return pl.pallas_call(
        functools.partial(_attn_body, tq=tq, n_meta=N_META),
        grid=(bsz, N_HEADS, tlen // tq),
        in_specs=[pl.BlockSpec(memory_space=pltpu.SMEM), q_spec, kv_spec, kv_spec],
        out_specs=q_spec,
        out_shape=jax.ShapeDtypeStruct((bsz, tlen, DIFF_W), F32),
        compiler_params=_cparams(("arbitrary", "arbitrary", "arbitrary")),
        name="diff_attn_prompt",
    )(lam, dq, kb, vb)


def _attn_paged_body(pt_ref, lam_ref, q_ref, kn_ref, vn_ref, *rest, n_pages, page, tpad, tnew):
    del pt_ref
    k_refs = rest[:n_pages]
    v_refs = rest[n_pages:2 * n_pages]
    o_ref, m_ref, l_ref, acc_ref = rest[2 * n_pages:]
    g = pl.program_id(1)
    n_rows = 2 * N_HEADS * tpad
    q = q_ref[0]
    own = _iota_div((n_rows, DIFF_W), 0, tpad) == _iota_div((n_rows, DIFF_W), 1, DIFF_DQK)
    qbd = jnp.where(own, jnp.concatenate([q] * (2 * N_HEADS), axis=0), jnp.zeros((), BF16))

    def update(s, v):
        m = m_ref[...]
        m_new = jnp.maximum(m, jnp.max(s, axis=-1, keepdims=True))
        alpha = jnp.exp(m - m_new)
        p = jnp.exp(s - m_new)
        l_ref[...] = alpha * l_ref[...] + jnp.sum(p, axis=-1, keepdims=True)
        acc_ref[...] = alpha * acc_ref[...] + _dot(p.astype(BF16), v)
        m_ref[...] = m_new

    @pl.when(g == 0)
    def _():
        m_ref[...] = jnp.full_like(m_ref, NEG_BIG)
        l_ref[...] = jnp.zeros_like(l_ref)
        acc_ref[...] = jnp.zeros_like(acc_ref)
        s = _dot_nt(qbd, kn_ref[0])
        tok = _iota_mod((n_rows, tpad), 0, tpad)
        col = lax.broadcasted_iota(I32, (n_rows, tpad), 1)
        s = jnp.where((col <= tok) & (col < tnew), s, NEG_BIG)
        update(s, vn_ref[0])

    k = jnp.concatenate([r[0].astype(BF16) for r in k_refs], axis=0)
    v = jnp.concatenate([r[0].astype(BF16) for r in v_refs], axis=0)
    update(_dot_nt(qbd, k), v)

    @pl.when(g == pl.num_programs(1) - 1)
    def _():
        o = acc_ref[...] / l_ref[...]
        lam = lam_ref[0]
        for h in range(N_HEADS):
            cols = slice(DIFF_DV * h, DIFF_DV * (h + 1))
            r1 = slice((2 * h) * tpad, (2 * h + 1) * tpad)
            r2 = slice((2 * h + 1) * tpad, (2 * h + 2) * tpad)
            o_ref[0, :, cols] = o[r1, cols] - lam * o[r2, cols]


def _attn_paged(page_table, lam, dq, kn, vn, cache_k, cache_v, *, tnew):
    dbsz, tpad, _ = dq.shape
    n_tbl = page_table.shape[1]
    page = cache_k.shape[1]
    n_pages = PAGES_PER_STEP
    while n_tbl % n_pages:
        n_pages //= 2
    n_rows = 2 * N_HEADS * tpad
    tok_spec = pl.BlockSpec((1, tpad, DIFF_W), lambda b, g, pt: (b, 0, 0))

    def page_spec(j):
        return pl.BlockSpec((1, page, DIFF_W), lambda b, g, pt: (pt[b, g * n_pages + j], 0, 0))

    grid_spec = pltpu.PrefetchScalarGridSpec(
        num_scalar_prefetch=1,
        grid=(dbsz, n_tbl // n_pages),
        in_specs=[pl.BlockSpec(memory_space=pltpu.SMEM), tok_spec, tok_spec, tok_spec]
        + [page_spec(j) for j in range(n_pages)] * 2,
        out_specs=tok_spec,
        scratch_shapes=[pltpu.VMEM((n_rows, 1), F32), pltpu.VMEM((n_rows, 1), F32),
                        pltpu.VMEM((n_rows, DIFF_W), F32)],
    )
    return pl.pallas_call(
        functools.partial(_attn_paged_body, n_pages=n_pages, page=page, tpad=tpad, tnew=tnew),
        grid_spec=grid_spec,
        out_shape=jax.ShapeDtypeStruct((dbsz, tpad, DIFF_W), F32),
        compiler_params=_cparams(("arbitrary", "arbitrary")),
        name="diff_attn_paged",
    )(page_table, lam, dq, kn, vn, *([cache_k] * n_pages), *([cache_v] * n_pages))


def _mix_body(og_ref, gg_ref, od_ref, x_ref, ggain_ref, dgain_ref, wo_ref, n2_ref, wrh_ref, wrl_ref, br_ref,
              h_ref, xn_ref, ei_ref, gt_ref):
    def head_norm(z, gain):
        parts = []
        for h in range(N_HEADS):
            seg = z[:, GLA_DV * h:GLA_DV * (h + 1)]
            parts.append(seg * lax.rsqrt(jnp.mean(seg * seg, axis=-1, keepdims=True) + EPS))
        return jnp.concatenate(parts, axis=-1) * gain

    gg = gg_ref[...]
    a = head_norm(og_ref[...], ggain_ref[...]) * (gg / (1.0 + jnp.exp(-gg)))
    d = head_norm(od_ref[...], dgain_ref[...]) * (1.0 - LAM_INIT)
    cat = jnp.concatenate([a, d], axis=-1).astype(BF16)
    hres = x_ref[...] + _dot(cat, wo_ref[...])
    h_ref[...] = hres

    xn = hres * lax.rsqrt(jnp.mean(hres * hres, axis=-1, keepdims=True) + EPS) * n2_ref[...]
    xn_ref[...] = xn

    x_hi, x_lo = _split_bf16(xn)
    logits = _dot(x_hi, wrh_ref[...]) + _dot(x_lo, wrh_ref[...]) + _dot(x_hi, wrl_ref[...]) + br_ref[...]
    lane = lax.broadcasted_iota(I32, logits.shape, 1).astype(F32)
    far = jnp.float32(1e4)

    def rmax(z):
        return jnp.max(z, axis=-1, keepdims=True)

    def first_lane(hit):
        return jnp.min(jnp.where(hit, lane, far), axis=-1, keepdims=True)

    is_group = lane < N_GROUPS
    lg = jnp.where(is_group, logits, NEG_BIG)
    mg = rmax(lg)
    grp = first_lane(lg == mg)
    g_gate = 1.0 / jnp.sum(jnp.where(is_group, jnp.exp(lg - mg), 0.0), axis=-1, keepdims=True)
    lo = N_GROUPS + EXPERTS_PER_GROUP * grp
    in_grp = (lane >= lo) & (lane < lo + EXPERTS_PER_GROUP)
    le = jnp.where(in_grp, logits, NEG_BIG)
    v1 = rmax(le)
    i1 = first_lane(in_grp & (le == v1))
    rest = in_grp & (lane != i1)
    le2 = jnp.where(rest, logits, NEG_BIG)
    v2 = rmax(le2)
    i2 = first_lane(rest & (le2 == v2))
    e21 = jnp.exp(v2 - v1)
    p1 = 1.0 / (1.0 + e21)
    ei = jnp.where(lane == 0, i1 - N_GROUPS, jnp.where(lane == 1, i2 - N_GROUPS, 0.0))
    ei_ref[...] = ei.astype(I32)
    gt_ref[...] = jnp.where(lane == 0, g_gate * p1, jnp.where(lane == 1, g_gate * (e21 * p1), 0.0))


def _mix_out(o_gla, gg, o_diff, x2d, prep, tile):
    n = x2d.shape[0]
    tm = _row_tile(n, tile)
    row = lambda w: pl.BlockSpec((tm, w), lambda i: (i, 0))
    full = lambda a: pl.BlockSpec(a.shape, lambda i: (0,) * a.ndim)
    consts = (prep["gla_gain"], prep["diff_gain"], prep["w_out"], prep["norm2"], prep["w_route_hi"],
              prep["w_route_lo"], prep["b_route"])
    return pl.pallas_call(
        _mix_body,
        grid=(n // tm,),
        in_specs=[row(GLA_WIDTH), row(GLA_WIDTH), row(DIFF_W), row(D_MODEL)] + [full(c) for c in consts],
        out_specs=[row(D_MODEL), row(D_MODEL), row(V7X_LANES), row(V7X_LANES)],
        out_shape=[jax.ShapeDtypeStruct((n, D_MODEL), F32), jax.ShapeDtypeStruct((n, D_MODEL), F32),
                   jax.ShapeDtypeStruct((n, V7X_LANES), I32), jax.ShapeDtypeStruct((n, V7X_LANES), F32)],
        compiler_params=_cparams(("arbitrary",)),
        name="mix_out_route",
    )(o_gla, gg, o_diff, x2d, *consts)


def _rank_body(ei_ref, rank_ref, cnt_ref, carry_ref):
    i = pl.program_id(0)

    @pl.when(i == 0)
    def _():
        carry_ref[...] = jnp.zeros_like(carry_ref)

    ei = ei_ref[...]
    tb = ei.shape[0]
    lane = lax.broadcasted_iota(I32, ei.shape, 1)
    e0 = ei[:, 0:1]
    e1 = ei[:, 1:2]
    oh0 = lane == e0
    oh1 = lane == e1
    cnt = oh0.astype(F32) + oh1.astype(F32)
    ri = lax.broadcasted_iota(I32, (tb, tb), 0)
    ci = lax.broadcasted_iota(I32, (tb, tb), 1)
    before = _dot((ci < ri).astype(BF16), cnt.astype(BF16)) + carry_ref[0:1, :]
    r0 = jnp.sum(jnp.where(oh0, before, 0.0), axis=-1, keepdims=True)
    r1 = jnp.sum(jnp.where(oh1, before, 0.0), axis=-1, keepdims=True)
    rank_ref[...] = jnp.where(lane == 0, r0, jnp.where(lane == 1, r1, 0.0)).astype(I32)
    total = carry_ref[0:1, :] + jnp.sum(cnt, axis=0, keepdims=True)
    carry_ref[...] = jnp.broadcast_to(total, carry_ref.shape)
    cnt_ref[...] = jnp.broadcast_to(total, cnt_ref.shape).astype(I32)


def _rank(ei):
    n = ei.shape[0]
    tb = _row_tile(n, 256)
    row = pl.BlockSpec((tb, V7X_LANES), lambda i: (i, 0))
    one = pl.BlockSpec((V7X_SUBLANES, V7X_LANES), lambda i: (0, 0))
    return pl.pallas_call(
        _rank_body,
        grid=(n // tb,),
        in_specs=[row],
        out_specs=[row, one],
        out_shape=[jax.ShapeDtypeStruct((n, V7X_LANES), I32), jax.ShapeDtypeStruct((V7X_SUBLANES, V7X_LANES), I32)],
        scratch_shapes=[pltpu.VMEM((V7X_SUBLANES, V7X_LANES), F32)],
        compiler_params=_cparams(("arbitrary",)),
        name="moe_rank",
    )(ei)


def _dispatch_body(dest_ref, x_ref, init_ref, xb_ref, sem, *, tb):
    del init_ref
    i = pl.program_id(0)

    def copy(t, k):
        return pltpu.make_async_copy(x_ref.at[pl.ds(i * tb + t, 1)], xb_ref.at[pl.ds(dest_ref[0, 0, 2 * t + k], 1)], sem)

    def start(t, c):
        copy(t, 0).start()
        copy(t, 1).start()
        return c

    lax.fori_loop(0, tb, start, 0)

    def wait(t, c):
        copy(t, 0).wait()
        copy(t, 1).wait()
        return c

    lax.fori_loop(0, tb, wait, 0)


def _dispatch(dest, xn, cap):
    n = xn.shape[0]
    tb = _row_tile(n, 512)
    dest3 = dest.reshape(n // tb, 1, 2 * tb)
    return pl.pallas_call(
        functools.partial(_dispatch_body, tb=tb),
        grid=(n // tb,),
        in_specs=[pl.BlockSpec((1, 1, 2 * tb), lambda i: (i, 0, 0), memory_space=pltpu.SMEM),
                  pl.BlockSpec(memory_space=pl.ANY), pl.BlockSpec(memory_space=pl.ANY)],
        out_specs=pl.BlockSpec(memory_space=pl.ANY),
        out_shape=jax.ShapeDtypeStruct((cap, D_MODEL), xn.dtype),
        input_output_aliases={2: 0},
        scratch_shapes=[pltpu.SemaphoreType.DMA(())],
        compiler_params=_cparams(("arbitrary",)),
        name="moe_dispatch",
    )(dest3, xn, jnp.zeros((cap, D_MODEL), xn.dtype))


def _expert_body(be_ref, bv_ref, xb_ref, w1_ref, w3_ref, w2_ref, yb_ref, w1b, w3b, w2b):
    i = pl.program_id(0)
    prev = be_ref[jnp.maximum(i - 1, 0)]

    @pl.when((i == 0) | (be_ref[i] != prev))
    def _():
        w1b[...] = w1_ref[0].astype(BF16)
        w3b[...] = w3_ref[0].astype(BF16)
        w2b[...] = w2_ref[0].astype(BF16)

    @pl.when(bv_ref[i] > 0)
    def _():
        x = xb_ref[...].astype(BF16)
        up = _dot(x, w1b[...])
        hid = (up / (1.0 + jnp.exp(-up))) * _dot(x, w3b[...])
        yb_ref[...] = _dot(hid.astype(BF16), w2b[...])

    @pl.when(bv_ref[i] == 0)
    def _():
        yb_ref[...] = jnp.zeros_like(yb_ref)


def _experts(blk_expert, blk_valid, xb, w_up, w_gate, w_down):
    cap = xb.shape[0]
    n_blk = cap // MOE_BLOCK
    w13 = pl.BlockSpec((1, D_MODEL, D_EXPERT), lambda i, be, bv: (be[i], 0, 0))
    w2 = pl.BlockSpec((1, D_EXPERT, D_MODEL), lambda i, be, bv: (be[i], 0, 0))
    rows = pl.BlockSpec((MOE_BLOCK, D_MODEL), lambda i, be, bv: (i, 0))
    grid_spec = pltpu.PrefetchScalarGridSpec(
        num_scalar_prefetch=2,
        grid=(n_blk,),
        in_specs=[rows, w13, w13, w2],
        out_specs=rows,
        scratch_shapes=[pltpu.VMEM((D_MODEL, D_EXPERT), BF16), pltpu.VMEM((D_MODEL, D_EXPERT), BF16),
                        pltpu.VMEM((D_EXPERT, D_MODEL), BF16)],
    )
    return pl.pallas_call(
        _expert_body,
        grid_spec=grid_spec,
        out_shape=jax.ShapeDtypeStruct((cap, D_MODEL), F32),
        compiler_params=_cparams(("arbitrary",)),
        name="moe_experts",
    )(blk_expert, blk_valid, xb, w_up, w_gate, w_down)


def _combine_body(dest_ref, h_ref, gt_ref, yb_ref, o_ref, buf, sem, *, tb):
    def copy(t, k):
        return pltpu.make_async_copy(yb_ref.at[pl.ds(dest_ref[0, 0, 2 * t + k], 1)], buf.at[k, pl.ds(t, 1)], sem)

    def start(t, c):
        copy(t, 0).start()
        copy(t, 1).start()
        return c

    lax.fori_loop(0, tb, start, 0)

    def wait(t, c):
        copy(t, 0).wait()
        copy(t, 1).wait()
        return c

    lax.fori_loop(0, tb, wait, 0)
    gt = gt_ref[...]
    o_ref[...] = h_ref[...] + gt[:, 0:1] * buf[0] + gt[:, 1:2] * buf[1]


def _combine(dest, hres, gt, yb):
    n = hres.shape[0]
    tb = _row_tile(n, 256)
    dest3 = dest.reshape(n // tb, 1, 2 * tb)
    return pl.pallas_call(
        functools.partial(_combine_body, tb=tb),
        grid=(n // tb,),
        in_specs=[pl.BlockSpec((1, 1, 2 * tb), lambda i: (i, 0, 0), memory_space=pltpu.SMEM),
                  pl.BlockSpec((tb, D_MODEL), lambda i: (i, 0)),
                  pl.BlockSpec((tb, V7X_LANES), lambda i: (i, 0)),
                  pl.BlockSpec(memory_space=pl.ANY)],
        out_specs=pl.BlockSpec((tb, D_MODEL), lambda i: (i, 0)),
        out_shape=jax.ShapeDtypeStruct((n, D_MODEL), F32),
        scratch_shapes=[pltpu.VMEM((2, tb, D_MODEL), F32), pltpu.SemaphoreType.DMA(())],
        compiler_params=_cparams(("arbitrary",)),
        name="moe_combine",
    )(dest3, hres, gt, yb)


def _moe(hres, xn, ei, gt, w_up, w_gate, w_down):
    n = hres.shape[0]
    rank, counts = _rank(ei)
    counts = counts[0, :N_EXPERTS]
    padded = (counts + MOE_BLOCK - 1) // MOE_BLOCK * MOE_BLOCK
    pad_end = jnp.cumsum(padded)
    pad_start = pad_end - padded
    e2 = ei[:, :2]
    dest = (pad_start[e2] + rank[:, :2]).reshape(-1)
    n_blk = -(-(2 * n + N_EXPERTS * (MOE_BLOCK - 1)) // MOE_BLOCK)
    blk_start = jnp.arange(n_blk, dtype=I32) * MOE_BLOCK
    blk_expert = jnp.minimum(jnp.searchsorted(pad_end, blk_start, side="right"), N_EXPERTS - 1).astype(I32)
    blk_valid = (blk_start < (pad_start + counts)[blk_expert]).astype(I32)
    last_used = jnp.max(jnp.where(blk_valid > 0, blk_expert, 0))
    blk_expert = jnp.where(blk_start < pad_end[-1], blk_expert, last_used)
    xb = _dispatch(dest, xn, n_blk * MOE_BLOCK)
    yb = _experts(blk_expert, blk_valid, xb, w_up, w_gate, w_down)
    return _combine(dest, hres, gt, yb)


def _prepare(norm1, w_in, w_gk2, b_gk2, gla_norm, q_norm, k_norm, diff_norm, w_out, norm2,
             w_route_group, b_route_group, w_route_expert, b_route_expert):
    gq, gk, gv, gg, glr, dq, dk, dv = jnp.split(
        w_in, [256, 512, 1024, 1536, 1552, 2064, 2576], axis=-1)
    glr = jnp.pad(glr, ((0, 0), (0, V7X_LANES - GLA_RANK)))
    w_packed = jnp.concatenate([gq, gk, gv, gg, dq, dk, dv, glr], axis=-1).astype(BF16)
    w_route = jnp.concatenate([w_route_group, w_route_expert], axis=-1)
    w_route = jnp.pad(w_route, ((0, 0), (0, V7X_LANES - N_GROUPS - N_EXPERTS)))
    w_route_hi = w_route.astype(BF16)
    b_route = jnp.pad(jnp.concatenate([b_route_group, b_route_expert]), (0, V7X_LANES - N_GROUPS - N_EXPERTS))
    gidx = jnp.arange(256) // DIFF_DQK
    return {
        "norm1": norm1.reshape(1, D_MODEL),
        "w_in": w_packed,
        "w_gk2": jnp.pad(w_gk2, ((0, V7X_LANES - GLA_RANK), (0, 0))).astype(BF16),
        "b_gk2": b_gk2.reshape(1, GLA_QK_W),
        "q_gain": jnp.tile(q_norm.reshape(-1), N_HEADS).reshape(1, DIFF_W),
        "k_gain": jnp.tile(k_norm.reshape(-1), N_HEADS).reshape(1, DIFF_W),
        "group_ones": (gidx[:, None] == gidx[None, :]).astype(BF16),
        "gla_gain": jnp.tile(gla_norm, N_HEADS).reshape(1, GLA_WIDTH),
        "diff_gain": jnp.tile(diff_norm, N_HEADS).reshape(1, DIFF_W),
        "w_out": w_out.astype(BF16),
        "norm2": norm2.reshape(1, D_MODEL),
        "w_route_hi": w_route_hi,
        "w_route_lo": (w_route - w_route_hi.astype(F32)).astype(BF16),
        "b_route": b_route.reshape(1, V7X_LANES),
    }


def kernel(x_prompt, x_sample, cache_k, cache_v, state_gla, page_table, meta_tokens, norm1, w_in, w_gk2, b_gk2,
           gla_norm, q_norm, k_norm, lam_q1, lam_k1, lam_q2, lam_k2, diff_norm, w_out, norm2, w_route_group,
           b_route_group, w_route_expert, b_route_expert, w_up, w_gate, w_down):
    bsz, seq, _ = x_prompt.shape
    dbsz, dseq, _ = x_sample.shape
    n_phys, page = cache_k.shape[:2]
    prep = _prepare(norm1[0], w_in[0], w_gk2[0], b_gk2[0], gla_norm[0], q_norm[0], k_norm[0], diff_norm[0],
                    w_out[0], norm2[0], w_route_group[0], b_route_group[0], w_route_expert[0], b_route_expert[0])
    lam = (jnp.exp(jnp.sum(lam_q1[0] * lam_k1[0])) - jnp.exp(jnp.sum(lam_q2[0] * lam_k2[0])) + LAM_INIT).reshape(1)
    w_up, w_gate, w_down = w_up[0], w_gate[0], w_down[0]

    mq, mk, mv, _, mglog, _, mdk, mdv, mdkb, mdvb = _inproj(meta_tokens, prep, N_META)
    s_zero = jnp.zeros((1, N_HEADS, GLA_DK, GLA_DV), F32)
    _, s_meta = _gla(mq[None], mk[None], mglog[None], mv[None], s_zero, chunk=N_META, n_sub=1)

    xp = x_prompt.reshape(bsz * seq, D_MODEL)
    gq, gk, gv, gg, glog, dq, dk, dv, dkb, dvb = _inproj(xp, prep, 512)
    seq3 = lambda a: a.reshape(bsz, seq, a.shape[-1])
    n_sub = 4 if seq % (4 * GLA_CHUNK) == 0 else 1
    o_gla, gla_prompt = _gla(seq3(gq), seq3(gk), seq3(glog), seq3(gv), s_meta, chunk=GLA_CHUNK, n_sub=n_sub)

    def with_meta(meta_rows, rows):
        meta_b = jnp.broadcast_to(meta_rows[None], (bsz,) + meta_rows.shape)
        return jnp.concatenate([meta_b, seq3(rows)], axis=1)

    k_prompt = with_meta(mdk, dk)
    v_prompt = with_meta(mdv, dv)
    o_diff = _attn_prompt(lam, seq3(dq), with_meta(mdkb, dkb), with_meta(mdvb, dvb), tq=_row_tile(seq, ATTN_TQ))
    hp, xnp, eip, gtp = _mix_out(o_gla.reshape(bsz * seq, GLA_WIDTH), gg, o_diff.reshape(bsz * seq, DIFF_W), xp,
                                 prep, 512)
    y_prompt = _moe(hp, xnp, eip, gtp, w_up, w_gate, w_down).reshape(bsz, seq, D_MODEL)

    tpad = -(-dseq // V7X_BF16_SUBLANES) * V7X_BF16_SUBLANES
    xs = jnp.pad(x_sample, ((0, 0), (0, tpad - dseq), (0, 0))).reshape(dbsz * tpad, D_MODEL)
    sq, sk, sv, sg, sglog, sdq, sdk, sdv, sdkb, sdvb = _inproj(xs, prep, 128)
    pad3 = lambda a: a.reshape(dbsz, tpad, a.shape[-1])
    unpad = lambda a: pad3(a)[:, :dseq]
    rows = lambda a: a.reshape(dbsz * dseq, a.shape[-1])
    o_gla_s, gla_sample = _gla(pad3(sq), pad3(sk), pad3(sglog), pad3(sv), state_gla[:, 0], chunk=tpad, n_sub=1,
                               valid=dseq)
    o_diff_s = _attn_paged(page_table, lam, pad3(sdq), pad3(sdkb), pad3(sdvb),
                           cache_k.reshape(n_phys, page, DIFF_W), cache_v.reshape(n_phys, page, DIFF_W), tnew=dseq)
    hs, xns, eis, gts = _mix_out(rows(unpad(o_gla_s)), rows(unpad(sg)), rows(unpad(o_diff_s)), rows(x_sample),
                                 prep, 128)
    y_sample = _moe(hs, xns, eis, gts, w_up, w_gate, w_down).reshape(dbsz, dseq, D_MODEL)

    heads = lambda a: a.reshape(a.shape[0], a.shape[1], 1, N_HEADS, DIFF_DV)
    return (y_prompt, y_sample, heads(k_prompt), heads(v_prompt), gla_prompt[:, None],
            heads(unpad(sdk)), heads(unpad(sdv)), gla_sample[:, None])
```

```python
import functools

import jax
import jax.numpy as jnp
from jax import lax
from jax.experimental import pallas as pl
from jax.experimental.pallas import tpu as pltpu

F32 = jnp.float32
BF16 = jnp.bfloat16
I32 = jnp.int32

V7X_LANES = 128
V7X_SUBLANES = 8
V7X_BF16_SUBLANES = 16
V7X_VMEM_LIMIT_BYTES = 56 * 1024 * 1024

D_MODEL = 1024
N_META = 16
N_HEADS = 4
GLA_DK = 64
GLA_DV = 128
GLA_RANK = 16
GLA_TAU = 16.0
GLA_QK_W = N_HEADS * GLA_DK
GLA_WIDTH = N_HEADS * GLA_DV
DIFF_DQK = 64
DIFF_DV = 128
DIFF_W = N_HEADS * DIFF_DV
N_GROUPS = 4
EXPERTS_PER_GROUP = 8
N_EXPERTS = N_GROUPS * EXPERTS_PER_GROUP
D_EXPERT = D_MODEL // 2
EPS = 1e-6
LAM_INIT = 0.2
NEG_BIG = -1e30
EXP_CLAMP = 80.0

OFF_GQ, OFF_GK, OFF_GV, OFF_GG = 0, 256, 512, 1024
OFF_DQ, OFF_DK, OFF_DV, OFF_GLR = 1536, 2048, 2560, 3072
PACKED_IN_W = OFF_GLR + V7X_LANES

GLA_CHUNK = 64
ATTN_TQ = 512
MOE_BLOCK = 256
PAGES_PER_STEP = 16
DMA_ISSUE_UNROLL = 8
SAFE_SCORE_BOUND = 40.0


def _cparams(sem):
    return pltpu.CompilerParams(dimension_semantics=sem, vmem_limit_bytes=V7X_VMEM_LIMIT_BYTES)


def _row_tile(n, pref):
    t = min(n, pref)
    while n % t:
        t //= 2
    return t


def _dot(a, b):
    return jnp.dot(a, b, preferred_element_type=F32)


def _dot_nt(a, b):
    return lax.dot_general(a, b, (((1,), (1,)), ((), ())), preferred_element_type=F32)


def _dot_tn(a, b):
    return lax.dot_general(a, b, (((0,), (0,)), ((), ())), preferred_element_type=F32)


def _iota_div(shape, dim, d):
    assert d & (d - 1) == 0
    return lax.shift_right_logical(lax.broadcasted_iota(I32, shape, dim), d.bit_length() - 1)


def _iota_mod(shape, dim, d):
    assert d & (d - 1) == 0
    return lax.broadcasted_iota(I32, shape, dim) & (d - 1)


def _split_bf16(x):
    hi = x.astype(BF16)
    lo = (x - hi.astype(F32)).astype(BF16)
    return hi, lo


def _inproj_body(x_ref, n1_ref, w_ref, wg2_ref, bg2_ref, qg_ref, kg_ref, gm_ref,
                 gq_ref, gk_ref, gv_ref, gg_ref, glog_ref, dq_ref, dk_ref, dv_ref, dkb_ref, dvb_ref):
    x = x_ref[...]
    ms = jnp.mean(x * x, axis=-1, keepdims=True)
    xn = (x * lax.rsqrt(ms + EPS) * n1_ref[...]).astype(BF16)

    def proj(off, width):
        return _dot(xn, w_ref[:, off:off + width])

    gq_ref[...] = proj(OFF_GQ, GLA_QK_W) * (GLA_DK ** -0.5)
    gk_ref[...] = proj(OFF_GK, GLA_QK_W)
    gv_ref[...] = proj(OFF_GV, GLA_WIDTH)
    gg_ref[...] = proj(OFF_GG, GLA_WIDTH)

    glr = proj(OFF_GLR, V7X_LANES).astype(BF16)
    gpre = _dot(glr, wg2_ref[...]) + bg2_ref[...]
    log_sig = jnp.minimum(gpre, 0.0) - jnp.log(1.0 + jnp.exp(-jnp.abs(gpre)))
    glog_ref[...] = log_sig * (1.0 / GLA_TAU)

    gm = gm_ref[...]

    def group_norm(z, gain):
        hi, lo = _split_bf16(z * z)
        parts = []
        for c in range(DIFF_W // 256):
            sl = slice(256 * c, 256 * (c + 1))
            parts.append(_dot(hi[:, sl], gm) + _dot(lo[:, sl], gm))
        ss = jnp.concatenate(parts, axis=-1)
        return z * lax.rsqrt(ss * (1.0 / DIFF_DQK) + EPS) * gain

    dq = group_norm(proj(OFF_DQ, DIFF_W), qg_ref[...])
    dq_ref[...] = (dq * (DIFF_DQK ** -0.5)).astype(BF16)
    dk = group_norm(proj(OFF_DK, DIFF_W), kg_ref[...])
    dk_ref[...] = dk
    dkb_ref[...] = dk.astype(BF16)
    dv = proj(OFF_DV, DIFF_W)
    dv_ref[...] = dv
    dvb_ref[...] = dv.astype(BF16)


def _inproj(x2d, prep, tile):
    n = x2d.shape[0]
    tm = _row_tile(n, tile)
    row = lambda w: pl.BlockSpec((tm, w), lambda i: (i, 0))
    full = lambda a: pl.BlockSpec(a.shape, lambda i: (0,) * a.ndim)
    consts = (prep["norm1"], prep["w_in"], prep["w_gk2"], prep["b_gk2"], prep["q_gain"], prep["k_gain"],
              prep["group_ones"])
    widths = (GLA_QK_W, GLA_QK_W, GLA_WIDTH, GLA_WIDTH, GLA_QK_W, DIFF_W, DIFF_W, DIFF_W, DIFF_W, DIFF_W)
    dtypes = (F32, F32, F32, F32, F32, BF16, F32, F32, BF16, BF16)
    return pl.pallas_call(
        _inproj_body,
        grid=(n // tm,),
        in_specs=[row(D_MODEL)] + [full(c) for c in consts],
        out_specs=[row(w) for w in widths],
        out_shape=[jax.ShapeDtypeStruct((n, w), d) for w, d in zip(widths, dtypes)],
        compiler_params=_cparams(("arbitrary",)),
        name="inproj",
    )(x2d, *consts)


def _gla_body(q_ref, k_ref, g_ref, v_ref, s0_ref, o_ref, sfin_ref, state_ref, *, chunk, n_sub, valid):
    t = pl.program_id(1)
    c = chunk

    @pl.when(t == 0)
    def _():
        state_ref[...] = jnp.zeros_like(state_ref)
        for h in range(N_HEADS):
            state_ref[GLA_DK * h:GLA_DK * (h + 1), GLA_DV * h:GLA_DV * (h + 1)] = s0_ref[0, h]

    ri = lax.broadcasted_iota(I32, (c, c), 0)
    ci = lax.broadcasted_iota(I32, (c, c), 1)
    tri = (ci <= ri).astype(BF16)
    ones_cols = jnp.ones((c, V7X_LANES), BF16)
    k_shape = (N_HEADS * c, GLA_QK_W)
    k_head_mask = _iota_div(k_shape, 0, c) == _iota_div(k_shape, 1, GLA_DK)
    v_shape = (N_HEADS * c, GLA_WIDTH)
    v_head_mask = _iota_div(v_shape, 0, c) == _iota_div(v_shape, 1, GLA_DV)
    s_shape = (GLA_QK_W, GLA_WIDTH)
    s_head_mask = _iota_div(s_shape, 0, GLA_DK) == _iota_div(s_shape, 1, GLA_DV)
    a_shape = (c, N_HEADS * c)
    causal = _iota_mod(a_shape, 1, c) <= lax.broadcasted_iota(I32, a_shape, 0)
    row_id = lax.broadcasted_iota(I32, (c, GLA_QK_W), 0)
    mid = c // 2 - 1

    for sub in range(n_sub):
        rows = slice(sub * c, (sub + 1) * c)
        q = q_ref[0, rows, :]
        k = k_ref[0, rows, :]
        g = g_ref[0, rows, :]
        v = v_ref[0, rows, :]
        if valid is not None:
            g = jnp.where(row_id < valid, g, 0.0)
        g_hi, g_lo = _split_bf16(g)
        b = _dot(tri, g_hi) + _dot(tri, g_lo)
        b_last_col = _dot_tn(g_hi, ones_cols) + _dot_tn(g_lo, ones_cols)
        b_last = b[c - 1:c, :]
        b_mid = b[mid:mid + 1, :]

        state = state_ref[...]
        q_dec = (q * jnp.exp(b)).astype(BF16)
        o_inter = _dot(q_dec, state.astype(BF16))

        q_t = (q * jnp.exp(jnp.minimum(b - b_mid, EXP_CLAMP))).astype(BF16)
        k_t = (k * jnp.exp(jnp.minimum(b_mid - b, EXP_CLAMP))).astype(BF16)
        k_rows = jnp.where(k_head_mask, jnp.concatenate([k_t] * N_HEADS, axis=0), 0)
        a = _dot_nt(q_t, k_rows)
        a = jnp.where(causal, a, 0.0).astype(BF16)
        v_bf = v.astype(BF16)
        v_rows = jnp.where(v_head_mask, jnp.concatenate([v_bf] * N_HEADS, axis=0), 0)
        o_ref[0, rows, :] = o_inter + _dot(a, v_rows)

        k_dec = (k * jnp.exp(b_last - b)).astype(BF16)
        ds = _dot_tn(k_dec, v_bf)
        decay = jnp.exp(jnp.concatenate([b_last_col] * (GLA_WIDTH // V7X_LANES), axis=1))
        state_ref[...] = decay * state + jnp.where(s_head_mask, ds, 0.0)

    @pl.when(t == pl.num_programs(1) - 1)
    def _():
        for h in range(N_HEADS):
            sfin_ref[0, h] = state_ref[GLA_DK * h:GLA_DK * (h + 1), GLA_DV * h:GLA_DV * (h + 1)]


def _gla(gq, gk, glog, gv, s0, *, chunk, n_sub, valid=None):
    bsz, tlen, _ = gq.shape
    tb = chunk * n_sub
    assert tlen % tb == 0
    s0_map = (lambda b, t: (b, 0, 0, 0)) if s0.shape[0] == bsz else (lambda b, t: (0, 0, 0, 0))
    seq = lambda w: pl.BlockSpec((1, tb, w), lambda b, t: (b, t, 0))
    st = (1, N_HEADS, GLA_DK, GLA_DV)
    return pl.pallas_call(
        functools.partial(_gla_body, chunk=chunk, n_sub=n_sub, valid=valid),
        grid=(bsz, tlen // tb),
        in_specs=[seq(GLA_QK_W), seq(GLA_QK_W), seq(GLA_QK_W), seq(GLA_WIDTH), pl.BlockSpec(st, s0_map)],
        out_specs=[seq(GLA_WIDTH), pl.BlockSpec(st, lambda b, t: (b, 0, 0, 0))],
        out_shape=[jax.ShapeDtypeStruct((bsz, tlen, GLA_WIDTH), F32),
                   jax.ShapeDtypeStruct((bsz, N_HEADS, GLA_DK, GLA_DV), F32)],
        scratch_shapes=[pltpu.VMEM((GLA_QK_W, GLA_WIDTH), F32)],
        compiler_params=_cparams(("arbitrary", "arbitrary")),
        name="gla_scan",
    )(gq, gk, glog, gv, s0)


def _attn_body(lam_ref, bound_ref, q_ref, k_ref, v_ref, o_ref, l_ref, acc_ref, *, tq, n_meta):
    i = pl.program_id(2)
    q = q_ref[0]
    lane = lax.broadcasted_iota(I32, (tq, DIFF_DV), 1)
    zero = jnp.zeros_like(q)
    qs = jnp.concatenate([jnp.where(lane < DIFF_DQK, q, zero), jnp.where(lane >= DIFF_DQK, q, zero)], axis=0)
    diag_mask = lax.broadcasted_iota(I32, (2 * tq, tq), 1) <= _iota_mod((2 * tq, tq), 0, tq)
    diag_start = pl.multiple_of(n_meta + i * tq, 16)
    bound = bound_ref[0]

    def scores(start, size):
        return _dot_nt(qs, k_ref[0, pl.ds(start, size), :])

    def finish(acc, l):
        o = acc / l
        o_ref[0] = o[:tq] - lam_ref[0] * o[tq:]

    @pl.when(bound <= SAFE_SCORE_BOUND)
    def _():
        def chunk(start, size, mask):
            p = jnp.exp(scores(start, size) - bound)
            if mask is not None:
                p = jnp.where(mask, p, 0.0)
            pv = _dot(p.astype(BF16), v_ref[0, pl.ds(start, size), :])
            return p, pv

        p, pv = chunk(0, n_meta, None)
        l_ref[...] = jnp.concatenate([p, jnp.zeros((2 * tq, V7X_LANES - n_meta), F32)], axis=-1)
        acc_ref[...] = pv

        def fold(p):
            return sum(p[:, V7X_LANES * c:V7X_LANES * (c + 1)] for c in range(tq // V7X_LANES))

        def full_chunk(j, carry):
            p, pv = chunk(pl.multiple_of(n_meta + j * tq, 16), tq, None)
            l_ref[...] += fold(p)
            acc_ref[...] += pv
            return carry

        lax.fori_loop(0, i, full_chunk, 0)
        p, pv = chunk(diag_start, tq, diag_mask)
        finish(acc_ref[...] + pv, jnp.sum(l_ref[...] + fold(p), axis=-1, keepdims=True))

    @pl.when(bound > SAFE_SCORE_BOUND)
    def _():
        def step(start, size, carry, mask):
            m, l, acc = carry
            s = scores(start, size)
            if mask is not None:
                s = jnp.where(mask, s, NEG_BIG)
            m_new = jnp.maximum(m, jnp.max(s, axis=-1, keepdims=True))
            alpha = jnp.exp(m - m_new)
            p = jnp.exp(s - m_new)
            l = alpha * l + jnp.sum(p, axis=-1, keepdims=True)
            acc = alpha * acc + _dot(p.astype(BF16), v_ref[0, pl.ds(start, size), :])
            return m_new, l, acc

        carry = (jnp.full((2 * tq, 1), NEG_BIG, F32), jnp.zeros((2 * tq, 1), F32),
                 jnp.zeros((2 * tq, DIFF_DV), F32))
        carry = step(0, n_meta, carry, None)
        carry = lax.fori_loop(
            0, i, lambda j, c: step(pl.multiple_of(n_meta + j * tq, 16), tq, c, None), carry)
        _, l, acc = step(diag_start, tq, carry, diag_mask)
        finish(acc, l)


def _attn_prompt(lam, bound, dq, kb, vb, *, tq):
    bsz, tlen, _ = dq.shape
    tk = kb.shape[1]
    assert tlen % tq == 0 and tq % V7X_LANES == 0 and tk == tlen + N_META
    kv_spec = pl.BlockSpec((1, tk, DIFF_DV), lambda b, h, i: (b, 0, h))
    q_spec = pl.BlockSpec((1, tq, DIFF_DV), lambda b, h, i: (b, i, h))
    smem = pl.BlockSpec(memory_space=pltpu.SMEM)
    return pl.pallas_call(
        functools.partial(_attn_body, tq=tq, n_meta=N_META),
        grid=(bsz, N_HEADS, tlen // tq),
        in_specs=[smem, smem, q_spec, kv_spec, kv_spec],
        out_specs=q_spec,
        out_shape=jax.ShapeDtypeStruct((bsz, tlen, DIFF_W), F32),
        scratch_shapes=[pltpu.VMEM((2 * tq, V7X_LANES), F32), pltpu.VMEM((2 * tq, DIFF_DV), F32)],
        compiler_params=_cparams(("arbitrary", "arbitrary", "arbitrary")),
        name="diff_attn_prompt",
    )(lam, bound, dq, kb, vb)


def _attn_paged_body(pt_ref, lam_ref, q_ref, kn_ref, vn_ref, *rest, n_pages, qrows, tnew):
    del pt_ref
    k_refs = rest[:n_pages]
    v_refs = rest[n_pages:2 * n_pages]
    o_ref, m_ref, l_ref, acc_ref = rest[2 * n_pages:]
    g = pl.program_id(1)
    n_rows = 2 * N_HEADS * qrows
    tpad = kn_ref.shape[1]
    q = q_ref[0]
    lane = lax.broadcasted_iota(I32, (qrows, DIFF_DV), 1)
    blocks = []
    for h in range(N_HEADS):
        qh = q[0:qrows, DIFF_DV * h:DIFF_DV * (h + 1)]
        blocks += [jnp.where(lane < DIFF_DQK, qh, 0.0), jnp.where(lane >= DIFF_DQK, qh, 0.0)]
    qr = jnp.concatenate(blocks, axis=0).astype(BF16)

    def update(s, v):
        m = m_ref[...]
        m_new = jnp.maximum(m, jnp.max(s, axis=-1, keepdims=True))
        alpha = jnp.exp(m - m_new)
        p = jnp.exp(s - m_new)
        l_ref[...] = alpha * l_ref[...] + jnp.sum(p, axis=-1, keepdims=True)
        acc_ref[...] = alpha * acc_ref[...] + _dot(p.astype(BF16), v)
        m_ref[...] = m_new

    @pl.when(g == 0)
    def _():
        m_ref[...] = jnp.full_like(m_ref, NEG_BIG)
        l_ref[...] = jnp.zeros_like(l_ref)
        acc_ref[...] = jnp.zeros_like(acc_ref)
        kn = jnp.concatenate([kn_ref[0, :, DIFF_DV * h:DIFF_DV * (h + 1)] for h in range(N_HEADS)], axis=0)
        vn = jnp.concatenate([vn_ref[0, :, DIFF_DV * h:DIFF_DV * (h + 1)] for h in range(N_HEADS)], axis=0)
        shape = (n_rows, N_HEADS * tpad)
        same_head = _iota_div(shape, 0, 2 * qrows) == _iota_div(shape, 1, tpad)
        tok = _iota_mod(shape, 1, tpad)
        visible = same_head & (tok <= _iota_mod(shape, 0, qrows)) & (tok < tnew)
        update(jnp.where(visible, _dot_nt(qr, kn), NEG_BIG), vn)

    k = jnp.concatenate([r[...].astype(BF16) for r in k_refs], axis=0)
    v = jnp.concatenate([r[...].astype(BF16) for r in v_refs], axis=0)
    tile = (n_rows, V7X_LANES)
    head_bias = jnp.where(_iota_mod(tile, 1, N_HEADS) == _iota_div(tile, 0, 2 * qrows), 0.0, NEG_BIG)
    update(_dot_nt(qr, k) + jnp.concatenate([head_bias] * (k.shape[0] // V7X_LANES), axis=1), v)

    @pl.when(g == pl.num_programs(1) - 1)
    def _():
        o = acc_ref[...] / l_ref[...]
        lam = lam_ref[0]
        for h in range(N_HEADS):
            r0 = 2 * qrows * h
            o_ref[0, :, DIFF_DV * h:DIFF_DV * (h + 1)] = o[r0:r0 + qrows] - lam * o[r0 + qrows:r0 + 2 * qrows]


def _attn_paged(page_table, lam, dq, kn, vn, cache_k, cache_v, *, page, tnew):
    dbsz, tpad, _ = dq.shape
    n_tbl = page_table.shape[1]
    qrows = V7X_SUBLANES
    assert tnew <= qrows <= tpad
    n_pages = PAGES_PER_STEP
    while n_tbl % n_pages:
        n_pages //= 2
    n_rows = 2 * N_HEADS * qrows
    tok_spec = pl.BlockSpec((1, tpad, DIFF_W), lambda b, g, pt: (b, 0, 0))
    out_spec = pl.BlockSpec((1, qrows, DIFF_W), lambda b, g, pt: (b, 0, 0))

    def page_spec(j):
        return pl.BlockSpec((page * N_HEADS, DIFF_DV), lambda b, g, pt: (pt[b, g * n_pages + j], 0))

    grid_spec = pltpu.PrefetchScalarGridSpec(
        num_scalar_prefetch=1,
        grid=(dbsz, n_tbl // n_pages),
        in_specs=[pl.BlockSpec(memory_space=pltpu.SMEM), tok_spec, tok_spec, tok_spec]
        + [page_spec(j) for j in range(n_pages)] * 2,
        out_specs=out_spec,
        scratch_shapes=[pltpu.VMEM((n_rows, 1), F32), pltpu.VMEM((n_rows, 1), F32),
                        pltpu.VMEM((n_rows, DIFF_DV), F32)],
    )
    return pl.pallas_call(
        functools.partial(_attn_paged_body, n_pages=n_pages, qrows=qrows, tnew=tnew),
        grid_spec=grid_spec,
        out_shape=jax.ShapeDtypeStruct((dbsz, qrows, DIFF_W), F32),
        compiler_params=_cparams(("arbitrary", "arbitrary")),
        name="diff_attn_paged",
    )(page_table, lam, dq, kn, vn, *([cache_k] * n_pages), *([cache_v] * n_pages))


def _mix_body(og_ref, gg_ref, od_ref, x_ref, ggain_ref, dgain_ref, wo_ref, n2_ref, wrh_ref, wrl_ref, br_ref,
              h_ref, xn_ref, ei_ref, gt_ref):
    def head_norm(z, gain):
        parts = []
        for h in range(N_HEADS):
            seg = z[:, GLA_DV * h:GLA_DV * (h + 1)]
            parts.append(seg * lax.rsqrt(jnp.mean(seg * seg, axis=-1, keepdims=True) + EPS))
        return jnp.concatenate(parts, axis=-1) * gain

    gg = gg_ref[...]
    a = head_norm(og_ref[...], ggain_ref[...]) * (gg / (1.0 + jnp.exp(-gg)))
    d = head_norm(od_ref[...], dgain_ref[...]) * (1.0 - LAM_INIT)
    cat = jnp.concatenate([a, d], axis=-1).astype(BF16)
    hres = x_ref[...] + _dot(cat, wo_ref[...])
    h_ref[...] = hres

    xn = hres * lax.rsqrt(jnp.mean(hres * hres, axis=-1, keepdims=True) + EPS) * n2_ref[...]
    xn_ref[...] = xn

    x_hi, x_lo = _split_bf16(xn)
    logits = _dot(x_hi, wrh_ref[...]) + _dot(x_lo, wrh_ref[...]) + _dot(x_hi, wrl_ref[...]) + br_ref[...]
    lane = lax.broadcasted_iota(I32, logits.shape, 1).astype(F32)
    far = jnp.float32(1e4)

    def rmax(z):
        return jnp.max(z, axis=-1, keepdims=True)

    def first_lane(hit):
        return jnp.min(jnp.where(hit, lane, far), axis=-1, keepdims=True)

    is_group = lane < N_GROUPS
    lg = jnp.where(is_group, logits, NEG_BIG)
    mg = rmax(lg)
    grp = first_lane(lg == mg)
    g_gate = 1.0 / jnp.sum(jnp.where(is_group, jnp.exp(lg - mg), 0.0), axis=-1, keepdims=True)
    lo = N_GROUPS + EXPERTS_PER_GROUP * grp
    in_grp = (lane >= lo) & (lane < lo + EXPERTS_PER_GROUP)
    le = jnp.where(in_grp, logits, NEG_BIG)
    v1 = rmax(le)
    i1 = first_lane(in_grp & (le == v1))
    rest = in_grp & (lane != i1)
    le2 = jnp.where(rest, logits, NEG_BIG)
    v2 = rmax(le2)
    i2 = first_lane(rest & (le2 == v2))
    e21 = jnp.exp(v2 - v1)
    p1 = 1.0 / (1.0 + e21)
    ei = jnp.where(lane == 0, i1 - N_GROUPS, jnp.where(lane == 1, i2 - N_GROUPS, 0.0))
    ei_ref[...] = ei.astype(I32)
    gt_ref[...] = jnp.where(lane == 0, g_gate * p1, jnp.where(lane == 1, g_gate * (e21 * p1), 0.0))


def _mix_out(o_gla, gg, o_diff, x2d, prep, tile):
    n = x2d.shape[0]
    tm = _row_tile(n, tile)
    row = lambda w: pl.BlockSpec((tm, w), lambda i: (i, 0))
    full = lambda a: pl.BlockSpec(a.shape, lambda i: (0,) * a.ndim)
    consts = (prep["gla_gain"], prep["diff_gain"], prep["w_out"], prep["norm2"], prep["w_route_hi"],
              prep["w_route_lo"], prep["b_route"])
    return pl.pallas_call(
        _mix_body,
        grid=(n // tm,),
        in_specs=[row(GLA_WIDTH), row(GLA_WIDTH), row(DIFF_W), row(D_MODEL)] + [full(c) for c in consts],
        out_specs=[row(D_MODEL), row(D_MODEL), row(V7X_LANES), row(V7X_LANES)],
        out_shape=[jax.ShapeDtypeStruct((n, D_MODEL), F32), jax.ShapeDtypeStruct((n, D_MODEL), F32),
                   jax.ShapeDtypeStruct((n, V7X_LANES), I32), jax.ShapeDtypeStruct((n, V7X_LANES), F32)],
        compiler_params=_cparams(("arbitrary",)),
        name="mix_out_route",
    )(o_gla, gg, o_diff, x2d, *consts)


def _rank_body(ei_ref, rank_ref, cnt_ref, carry_ref):
    i = pl.program_id(0)

    @pl.when(i == 0)
    def _():
        carry_ref[...] = jnp.zeros_like(carry_ref)

    ei = ei_ref[...]
    tb = ei.shape[0]
    lane = lax.broadcasted_iota(I32, ei.shape, 1)
    e0 = ei[:, 0:1]
    e1 = ei[:, 1:2]
    oh0 = lane == e0
    oh1 = lane == e1
    cnt = oh0.astype(F32) + oh1.astype(F32)
    ri = lax.broadcasted_iota(I32, (tb, tb), 0)
    ci = lax.broadcasted_iota(I32, (tb, tb), 1)
    before = _dot((ci < ri).astype(BF16), cnt.astype(BF16)) + carry_ref[0:1, :]
    r0 = jnp.sum(jnp.where(oh0, before, 0.0), axis=-1, keepdims=True)
    r1 = jnp.sum(jnp.where(oh1, before, 0.0), axis=-1, keepdims=True)
    rank_ref[...] = jnp.where(lane == 0, r0, jnp.where(lane == 1, r1, 0.0)).astype(I32)
    total = carry_ref[0:1, :] + jnp.sum(cnt, axis=0, keepdims=True)
    carry_ref[...] = jnp.broadcast_to(total, carry_ref.shape)
    cnt_ref[...] = jnp.broadcast_to(total, cnt_ref.shape).astype(I32)


def _rank(ei):
    n = ei.shape[0]
    tb = _row_tile(n, 256)
    row = pl.BlockSpec((tb, V7X_LANES), lambda i: (i, 0))
    one = pl.BlockSpec((V7X_SUBLANES, V7X_LANES), lambda i: (0, 0))
    return pl.pallas_call(
        _rank_body,
        grid=(n // tb,),
        in_specs=[row],
        out_specs=[row, one],
        out_shape=[jax.ShapeDtypeStruct((n, V7X_LANES), I32), jax.ShapeDtypeStruct((V7X_SUBLANES, V7X_LANES), I32)],
        scratch_shapes=[pltpu.VMEM((V7X_SUBLANES, V7X_LANES), F32)],
        compiler_params=_cparams(("arbitrary",)),
        name="moe_rank",
    )(ei)


def _gather_body(idx_ref, x_ref, xb_ref, sem, *, tb):
    def copy(r):
        return pltpu.make_async_copy(x_ref.at[pl.ds(idx_ref[0, 0, r], 1)], xb_ref.at[pl.ds(r, 1)], sem)

    def start(r, c):
        copy(r).start()
        return c

    lax.fori_loop(0, tb, start, 0, unroll=DMA_ISSUE_UNROLL)

    def wait(r, c):
        copy(r).wait()
        return c

    lax.fori_loop(0, tb, wait, 0, unroll=DMA_ISSUE_UNROLL)


def _gather_rows(idx, x):
    cap = idx.shape[0]
    tb = _row_tile(cap, 512)
    return pl.pallas_call(
        functools.partial(_gather_body, tb=tb),
        grid=(cap // tb,),
        in_specs=[pl.BlockSpec((1, 1, tb), lambda i: (i, 0, 0), memory_space=pltpu.SMEM),
                  pl.BlockSpec(memory_space=pl.ANY)],
        out_specs=pl.BlockSpec((tb, D_MODEL), lambda i: (i, 0)),
        out_shape=jax.ShapeDtypeStruct((cap, D_MODEL), x.dtype),
        scratch_shapes=[pltpu.SemaphoreType.DMA(())],
        compiler_params=_cparams(("arbitrary",)),
        name="moe_gather",
    )(idx.reshape(cap // tb, 1, tb), x)


def _expert_body(be_ref, bv_ref, xb_ref, w1_ref, w3_ref, w2_ref, yb_ref, w1b, w3b, w2b):
    i = pl.program_id(0)
    prev = be_ref[jnp.maximum(i - 1, 0)]

    @pl.when((i == 0) | (be_ref[i] != prev))
    def _():
        w1b[...] = w1_ref[0].astype(BF16)
        w3b[...] = w3_ref[0].astype(BF16)
        w2b[...] = w2_ref[0].astype(BF16)

    @pl.when(bv_ref[i] > 0)
    def _():
        x = xb_ref[...].astype(BF16)
        up = _dot(x, w1b[...])
        hid = (up / (1.0 + jnp.exp(-up))) * _dot(x, w3b[...])
        yb_ref[...] = _dot(hid.astype(BF16), w2b[...])

    @pl.when(bv_ref[i] == 0)
    def _():
        yb_ref[...] = jnp.zeros_like(yb_ref)


def _experts(blk_expert, blk_valid, xb, w_up, w_gate, w_down):
    cap = xb.shape[0]
    n_blk = cap // MOE_BLOCK
    w13 = pl.BlockSpec((1, D_MODEL, D_EXPERT), lambda i, be, bv: (be[i], 0, 0))
    w2 = pl.BlockSpec((1, D_EXPERT, D_MODEL), lambda i, be, bv: (be[i], 0, 0))
    rows = pl.BlockSpec((MOE_BLOCK, D_MODEL), lambda i, be, bv: (i, 0))
    grid_spec = pltpu.PrefetchScalarGridSpec(
        num_scalar_prefetch=2,
        grid=(n_blk,),
        in_specs=[rows, w13, w13, w2],
        out_specs=rows,
        scratch_shapes=[pltpu.VMEM((D_MODEL, D_EXPERT), BF16), pltpu.VMEM((D_MODEL, D_EXPERT), BF16),
                        pltpu.VMEM((D_EXPERT, D_MODEL), BF16)],
    )
    return pl.pallas_call(
        _expert_body,
        grid_spec=grid_spec,
        out_shape=jax.ShapeDtypeStruct((cap, D_MODEL), F32),
        compiler_params=_cparams(("arbitrary",)),
        name="moe_experts",
    )(blk_expert, blk_valid, xb, w_up, w_gate, w_down)


def _combine_body(dest_ref, h_ref, gt_ref, yb_ref, o_ref, buf, sem, *, tb):
    def copy(t, k):
        return pltpu.make_async_copy(yb_ref.at[pl.ds(dest_ref[0, 0, 2 * t + k], 1)], buf.at[k, pl.ds(t, 1)], sem)

    def start(t, c):
        copy(t, 0).start()
        copy(t, 1).start()
        return c

    lax.fori_loop(0, tb, start, 0, unroll=DMA_ISSUE_UNROLL)

    def wait(t, c):
        copy(t, 0).wait()
        copy(t, 1).wait()
        return c

    lax.fori_loop(0, tb, wait, 0, unroll=DMA_ISSUE_UNROLL)
    gt = gt_ref[...]
    o_ref[...] = h_ref[...] + gt[:, 0:1] * buf[0] + gt[:, 1:2] * buf[1]


def _combine(dest, hres, gt, yb):
    n = hres.shape[0]
    tb = _row_tile(n, 256)
    dest3 = dest.reshape(n // tb, 1, 2 * tb)
    return pl.pallas_call(
        functools.partial(_combine_body, tb=tb),
        grid=(n // tb,),
        in_specs=[pl.BlockSpec((1, 1, 2 * tb), lambda i: (i, 0, 0), memory_space=pltpu.SMEM),
                  pl.BlockSpec((tb, D_MODEL), lambda i: (i, 0)),
                  pl.BlockSpec((tb, V7X_LANES), lambda i: (i, 0)),
                  pl.BlockSpec(memory_space=pl.ANY)],
        out_specs=pl.BlockSpec((tb, D_MODEL), lambda i: (i, 0)),
        out_shape=jax.ShapeDtypeStruct((n, D_MODEL), F32),
        scratch_shapes=[pltpu.VMEM((2, tb, D_MODEL), F32), pltpu.SemaphoreType.DMA(())],
        compiler_params=_cparams(("arbitrary",)),
        name="moe_combine",
    )(dest3, hres, gt, yb)


def _moe(hres, xn, ei, gt, w_up, w_gate, w_down):
    n = hres.shape[0]
    rank, counts = _rank(ei)
    counts = counts[0, :N_EXPERTS]
    padded = (counts + MOE_BLOCK - 1) // MOE_BLOCK * MOE_BLOCK
    pad_end = jnp.cumsum(padded)
    pad_start = pad_end - padded
    e2 = ei[:, :2]
    dest = (pad_start[e2] + rank[:, :2]).reshape(-1)
    n_blk = -(-(2 * n + N_EXPERTS * (MOE_BLOCK - 1)) // MOE_BLOCK)
    blk_start = jnp.arange(n_blk, dtype=I32) * MOE_BLOCK
    blk_expert = jnp.sum((pad_end[None, :] <= blk_start[:, None]).astype(I32), axis=1)
    blk_expert = jnp.minimum(blk_expert, N_EXPERTS - 1)
    blk_valid = (blk_start < (pad_start + counts)[blk_expert]).astype(I32)
    last_used = jnp.max(jnp.where(blk_valid > 0, blk_expert, 0))
    blk_expert = jnp.where(blk_start < pad_end[-1], blk_expert, last_used)
    tok = jnp.arange(2 * n, dtype=I32) // 2
    slot_tok = jnp.zeros((n_blk * MOE_BLOCK,), I32).at[dest].set(tok, unique_indices=True)
    xb = _gather_rows(slot_tok, xn)
    yb = _experts(blk_expert, blk_valid, xb, w_up, w_gate, w_down)
    return _combine(dest, hres, gt, yb)


def _prepare(norm1, w_in, w_gk2, b_gk2, gla_norm, q_norm, k_norm, diff_norm, w_out, norm2,
             w_route_group, b_route_group, w_route_expert, b_route_expert):
    gq, gk, gv, gg, glr, dq, dk, dv = jnp.split(
        w_in, [256, 512, 1024, 1536, 1552, 2064, 2576], axis=-1)
    glr = jnp.pad(glr, ((0, 0), (0, V7X_LANES - GLA_RANK)))
    w_packed = jnp.concatenate([gq, gk, gv, gg, dq, dk, dv, glr], axis=-1).astype(BF16)
    w_route = jnp.concatenate([w_route_group, w_route_expert], axis=-1)
    w_route = jnp.pad(w_route, ((0, 0), (0, V7X_LANES - N_GROUPS - N_EXPERTS)))
    w_route_hi = w_route.astype(BF16)
    b_route = jnp.pad(jnp.concatenate([b_route_group, b_route_expert]), (0, V7X_LANES - N_GROUPS - N_EXPERTS))
    gidx = jnp.arange(256) // DIFF_DQK
    return {
        "norm1": norm1.reshape(1, D_MODEL),
        "w_in": w_packed,
        "w_gk2": jnp.pad(w_gk2, ((0, V7X_LANES - GLA_RANK), (0, 0))).astype(BF16),
        "b_gk2": b_gk2.reshape(1, GLA_QK_W),
        "q_gain": jnp.tile(q_norm.reshape(-1), N_HEADS).reshape(1, DIFF_W),
        "k_gain": jnp.tile(k_norm.reshape(-1), N_HEADS).reshape(1, DIFF_W),
        "group_ones": (gidx[:, None] == gidx[None, :]).astype(BF16),
        "gla_gain": jnp.tile(gla_norm, N_HEADS).reshape(1, GLA_WIDTH),
        "diff_gain": jnp.tile(diff_norm, N_HEADS).reshape(1, DIFF_W),
        "w_out": w_out.astype(BF16),
        "norm2": norm2.reshape(1, D_MODEL),
        "w_route_hi": w_route_hi,
        "w_route_lo": (w_route - w_route_hi.astype(F32)).astype(BF16),
        "b_route": b_route.reshape(1, V7X_LANES),
    }


def kernel(x_prompt, x_sample, cache_k, cache_v, state_gla, page_table, meta_tokens, norm1, w_in, w_gk2, b_gk2,
           gla_norm, q_norm, k_norm, lam_q1, lam_k1, lam_q2, lam_k2, diff_norm, w_out, norm2, w_route_group,
           b_route_group, w_route_expert, b_route_expert, w_up, w_gate, w_down):
    bsz, seq, _ = x_prompt.shape
    dbsz, dseq, _ = x_sample.shape
    n_phys, page = cache_k.shape[:2]
    prep = _prepare(norm1[0], w_in[0], w_gk2[0], b_gk2[0], gla_norm[0], q_norm[0], k_norm[0], diff_norm[0],
                    w_out[0], norm2[0], w_route_group[0], b_route_group[0], w_route_expert[0], b_route_expert[0])
    lam = (jnp.exp(jnp.sum(lam_q1[0] * lam_k1[0])) - jnp.exp(jnp.sum(lam_q2[0] * lam_k2[0])) + LAM_INIT).reshape(1)
    w_up, w_gate, w_down = w_up[0], w_gate[0], w_down[0]

    mq, mk, mv, _, mglog, _, mdk, mdv, mdkb, mdvb = _inproj(meta_tokens, prep, N_META)
    s_zero = jnp.zeros((1, N_HEADS, GLA_DK, GLA_DV), F32)
    _, s_meta = _gla(mq[None], mk[None], mglog[None], mv[None], s_zero, chunk=N_META, n_sub=1)

    xp = x_prompt.reshape(bsz * seq, D_MODEL)
    gq, gk, gv, gg, glog, dq, dk, dv, dkb, dvb = _inproj(xp, prep, 512)
    seq3 = lambda a: a.reshape(bsz, seq, a.shape[-1])
    n_sub = 4 if seq % (4 * GLA_CHUNK) == 0 else 1
    o_gla, gla_prompt = _gla(seq3(gq), seq3(gk), seq3(glog), seq3(gv), s_meta, chunk=GLA_CHUNK, n_sub=n_sub)

    def with_meta(meta_rows, rows):
        meta_b = jnp.broadcast_to(meta_rows[None], (bsz,) + meta_rows.shape)
        return jnp.concatenate([meta_b, seq3(rows)], axis=1)

    k_prompt = with_meta(mdk, dk)
    v_prompt = with_meta(mdv, dv)
    bound = (8.1 * jnp.max(jnp.abs(q_norm[0])) * jnp.max(jnp.abs(k_norm[0]))).reshape(1)
    o_diff = _attn_prompt(lam, bound, seq3(dq), with_meta(mdkb, dkb), with_meta(mdvb, dvb),
                          tq=_row_tile(seq, ATTN_TQ))
    hp, xnp, eip, gtp = _mix_out(o_gla.reshape(bsz * seq, GLA_WIDTH), gg, o_diff.reshape(bsz * seq, DIFF_W), xp,
                                 prep, 512)
    y_prompt = _moe(hp, xnp, eip, gtp, w_up, w_gate, w_down).reshape(bsz, seq, D_MODEL)

    tpad = -(-dseq // V7X_BF16_SUBLANES) * V7X_BF16_SUBLANES
    xs = jnp.pad(x_sample, ((0, 0), (0, tpad - dseq), (0, 0))).reshape(dbsz * tpad, D_MODEL)
    sq, sk, sv, sg, sglog, sdq, sdk, sdv, sdkb, sdvb = _inproj(xs, prep, 128)
    pad3 = lambda a: a.reshape(dbsz, tpad, a.shape[-1])
    unpad = lambda a: pad3(a)[:, :dseq]
    rows = lambda a: a.reshape(dbsz * dseq, a.shape[-1])
    o_gla_s, gla_sample = _gla(pad3(sq), pad3(sk), pad3(sglog), pad3(sv), state_gla[:, 0], chunk=tpad, n_sub=1,
                               valid=dseq)
    cache_rows = lambda c: c.reshape(n_phys * page * N_HEADS, DIFF_DV)
    o_diff_s = _attn_paged(page_table, lam, pad3(sdq).astype(F32), pad3(sdkb), pad3(sdvb), cache_rows(cache_k),
                           cache_rows(cache_v), page=page, tnew=dseq)
    hs, xns, eis, gts = _mix_out(rows(unpad(o_gla_s)), rows(unpad(sg)), rows(o_diff_s[:, :dseq]), rows(x_sample),
                                 prep, 128)
    y_sample = _moe(hs, xns, eis, gts, w_up, w_gate, w_down).reshape(dbsz, dseq, D_MODEL)

    heads = lambda a: a.reshape(a.shape[0], a.shape[1], 1, N_HEADS, DIFF_DV)
    return (y_prompt, y_sample, heads(k_prompt), heads(v_prompt), gla_prompt[:, None],
            heads(unpad(sdk)), heads(unpad(sdv)), gla_sample[:, None])
```

```python
import functools

import jax
import jax.numpy as jnp
from jax import lax
from jax.experimental import pallas as pl
from jax.experimental.pallas import tpu as pltpu

F32 = jnp.float32
BF16 = jnp.bfloat16
I32 = jnp.int32

V7X_LANES = 128
V7X_SUBLANES = 8
V7X_BF16_SUBLANES = 16
V7X_VMEM_LIMIT_BYTES = 56 * 1024 * 1024

D_MODEL = 1024
N_META = 16
N_HEADS = 4
GLA_DK = 64
GLA_DV = 128
GLA_RANK = 16
GLA_TAU = 16.0
GLA_QK_W = N_HEADS * GLA_DK
GLA_WIDTH = N_HEADS * GLA_DV
DIFF_DQK = 64
DIFF_DV = 128
DIFF_W = N_HEADS * DIFF_DV
N_GROUPS = 4
EXPERTS_PER_GROUP = 8
N_EXPERTS = N_GROUPS * EXPERTS_PER_GROUP
D_EXPERT = D_MODEL // 2
EPS = 1e-6
LAM_INIT = 0.2
NEG_BIG = -1e30
EXP_CLAMP = 80.0

OFF_GQ, OFF_GK, OFF_GV, OFF_GG = 0, 256, 512, 1024
OFF_DQ, OFF_DK, OFF_DV, OFF_GLR = 1536, 2048, 2560, 3072
PACKED_IN_W = OFF_GLR + V7X_LANES

GLA_CHUNK = 64
ATTN_TQ = 512
MOE_BLOCK = 256
MOE_TOKEN_TILE = 512
PAGES_PER_STEP = 16
DMA_ISSUE_UNROLL = 8
SAFE_SCORE_BOUND = 40.0


def _cparams(sem):
    return pltpu.CompilerParams(dimension_semantics=sem, vmem_limit_bytes=V7X_VMEM_LIMIT_BYTES)


def _row_tile(n, pref):
    t = min(n, pref)
    while n % t:
        t //= 2
    return t


def _dot(a, b):
    return jnp.dot(a, b, preferred_element_type=F32)


def _dot_nt(a, b):
    return lax.dot_general(a, b, (((1,), (1,)), ((), ())), preferred_element_type=F32)


def _dot_tn(a, b):
    return lax.dot_general(a, b, (((0,), (0,)), ((), ())), preferred_element_type=F32)


def _iota_div(shape, dim, d):
    assert d & (d - 1) == 0
    return lax.shift_right_logical(lax.broadcasted_iota(I32, shape, dim), d.bit_length() - 1)


def _iota_mod(shape, dim, d):
    assert d & (d - 1) == 0
    return lax.broadcasted_iota(I32, shape, dim) & (d - 1)


ROW_TILE = D_MODEL // V7X_LANES


def _store_row_tiles(ref, x):
    rows = x.shape[0]
    for s in range(ROW_TILE):
        ref[pl.ds(s, rows, stride=ROW_TILE), :] = x[:, V7X_LANES * s:V7X_LANES * (s + 1)]


def _load_row_tiles(ref, rows):
    return jnp.concatenate([ref[pl.ds(s, rows, stride=ROW_TILE), :] for s in range(ROW_TILE)], axis=1)


def _split_bf16(x):
    hi = x.astype(BF16)
    lo = (x - hi.astype(F32)).astype(BF16)
    return hi, lo


def _inproj_body(x_ref, n1_ref, w_ref, wg2_ref, bg2_ref, qg_ref, kg_ref, gm_ref,
                 gq_ref, gk_ref, gv_ref, gg_ref, glog_ref, dq_ref, dk_ref, dv_ref, dkb_ref, dvb_ref):
    x = x_ref[...]
    ms = jnp.mean(x * x, axis=-1, keepdims=True)
    xn = (x * lax.rsqrt(ms + EPS) * n1_ref[...]).astype(BF16)

    def proj(off, width):
        return _dot(xn, w_ref[:, off:off + width])

    gq_ref[...] = proj(OFF_GQ, GLA_QK_W) * (GLA_DK ** -0.5)
    gk_ref[...] = proj(OFF_GK, GLA_QK_W)
    gv_ref[...] = proj(OFF_GV, GLA_WIDTH)
    gg_ref[...] = proj(OFF_GG, GLA_WIDTH)

    glr = proj(OFF_GLR, V7X_LANES).astype(BF16)
    gpre = _dot(glr, wg2_ref[...]) + bg2_ref[...]
    log_sig = jnp.minimum(gpre, 0.0) - jnp.log(1.0 + jnp.exp(-jnp.abs(gpre)))
    glog_ref[...] = log_sig * (1.0 / GLA_TAU)

    gm = gm_ref[...]

    def group_norm(z, gain):
        hi, lo = _split_bf16(z * z)
        parts = []
        for c in range(DIFF_W // 256):
            sl = slice(256 * c, 256 * (c + 1))
            parts.append(_dot(hi[:, sl], gm) + _dot(lo[:, sl], gm))
        ss = jnp.concatenate(parts, axis=-1)
        return z * lax.rsqrt(ss * (1.0 / DIFF_DQK) + EPS) * gain

    dq = group_norm(proj(OFF_DQ, DIFF_W), qg_ref[...])
    dq_ref[...] = (dq * (DIFF_DQK ** -0.5)).astype(BF16)
    dk = group_norm(proj(OFF_DK, DIFF_W), kg_ref[...])
    dk_ref[...] = dk
    dkb_ref[...] = dk.astype(BF16)
    dv = proj(OFF_DV, DIFF_W)
    dv_ref[...] = dv
    dvb_ref[...] = dv.astype(BF16)


def _inproj(x2d, prep, tile):
    n = x2d.shape[0]
    tm = _row_tile(n, tile)
    row = lambda w: pl.BlockSpec((tm, w), lambda i: (i, 0))
    full = lambda a: pl.BlockSpec(a.shape, lambda i: (0,) * a.ndim)
    consts = (prep["norm1"], prep["w_in"], prep["w_gk2"], prep["b_gk2"], prep["q_gain"], prep["k_gain"],
              prep["group_ones"])
    widths = (GLA_QK_W, GLA_QK_W, GLA_WIDTH, GLA_WIDTH, GLA_QK_W, DIFF_W, DIFF_W, DIFF_W, DIFF_W, DIFF_W)
    dtypes = (F32, F32, F32, F32, F32, BF16, F32, F32, BF16, BF16)
    return pl.pallas_call(
        _inproj_body,
        grid=(n // tm,),
        in_specs=[row(D_MODEL)] + [full(c) for c in consts],
        out_specs=[row(w) for w in widths],
        out_shape=[jax.ShapeDtypeStruct((n, w), d) for w, d in zip(widths, dtypes)],
        compiler_params=_cparams(("arbitrary",)),
        name="inproj",
    )(x2d, *consts)


def _gla_body(q_ref, k_ref, g_ref, v_ref, s0_ref, o_ref, sfin_ref, state_ref, *, chunk, n_sub, valid):
    t = pl.program_id(1)
    c = chunk

    @pl.when(t == 0)
    def _():
        state_ref[...] = jnp.zeros_like(state_ref)
        for h in range(N_HEADS):
            state_ref[GLA_DK * h:GLA_DK * (h + 1), GLA_DV * h:GLA_DV * (h + 1)] = s0_ref[0, h]

    ri = lax.broadcasted_iota(I32, (c, c), 0)
    ci = lax.broadcasted_iota(I32, (c, c), 1)
    tri = (ci <= ri).astype(BF16)
    ones_cols = jnp.ones((c, V7X_LANES), BF16)
    k_shape = (N_HEADS * c, GLA_QK_W)
    k_head_mask = _iota_div(k_shape, 0, c) == _iota_div(k_shape, 1, GLA_DK)
    v_shape = (N_HEADS * c, GLA_WIDTH)
    v_head_mask = _iota_div(v_shape, 0, c) == _iota_div(v_shape, 1, GLA_DV)
    s_shape = (GLA_QK_W, GLA_WIDTH)
    s_head_mask = _iota_div(s_shape, 0, GLA_DK) == _iota_div(s_shape, 1, GLA_DV)
    a_shape = (c, N_HEADS * c)
    causal = _iota_mod(a_shape, 1, c) <= lax.broadcasted_iota(I32, a_shape, 0)
    row_id = lax.broadcasted_iota(I32, (c, GLA_QK_W), 0)
    mid = c // 2 - 1

    for sub in range(n_sub):
        rows = slice(sub * c, (sub + 1) * c)
        q = q_ref[0, rows, :]
        k = k_ref[0, rows, :]
        g = g_ref[0, rows, :]
        v = v_ref[0, rows, :]
        if valid is not None:
            g = jnp.where(row_id < valid, g, 0.0)
        g_hi, g_lo = _split_bf16(g)
        b = _dot(tri, g_hi) + _dot(tri, g_lo)
        b_last_col = _dot_tn(g_hi, ones_cols) + _dot_tn(g_lo, ones_cols)
        b_last = b[c - 1:c, :]
        b_mid = b[mid:mid + 1, :]

        state = state_ref[...]
        q_dec = (q * jnp.exp(b)).astype(BF16)
        o_inter = _dot(q_dec, state.astype(BF16))

        q_t = (q * jnp.exp(jnp.minimum(b - b_mid, EXP_CLAMP))).astype(BF16)
        k_t = (k * jnp.exp(jnp.minimum(b_mid - b, EXP_CLAMP))).astype(BF16)
        k_rows = jnp.where(k_head_mask, jnp.concatenate([k_t] * N_HEADS, axis=0), 0)
        a = _dot_nt(q_t, k_rows)
        a = jnp.where(causal, a, 0.0).astype(BF16)
        v_bf = v.astype(BF16)
        v_rows = jnp.where(v_head_mask, jnp.concatenate([v_bf] * N_HEADS, axis=0), 0)
        o_ref[0, rows, :] = o_inter + _dot(a, v_rows)

        k_dec = (k * jnp.exp(b_last - b)).astype(BF16)
        ds = _dot_tn(k_dec, v_bf)
        decay = jnp.exp(jnp.concatenate([b_last_col] * (GLA_WIDTH // V7X_LANES), axis=1))
        state_ref[...] = decay * state + jnp.where(s_head_mask, ds, 0.0)

    @pl.when(t == pl.num_programs(1) - 1)
    def _():
        for h in range(N_HEADS):
            sfin_ref[0, h] = state_ref[GLA_DK * h:GLA_DK * (h + 1), GLA_DV * h:GLA_DV * (h + 1)]


def _gla(gq, gk, glog, gv, s0, *, chunk, n_sub, valid=None):
    bsz, tlen, _ = gq.shape
    tb = chunk * n_sub
    assert tlen % tb == 0
    s0_map = (lambda b, t: (b, 0, 0, 0)) if s0.shape[0] == bsz else (lambda b, t: (0, 0, 0, 0))
    seq = lambda w: pl.BlockSpec((1, tb, w), lambda b, t: (b, t, 0))
    st = (1, N_HEADS, GLA_DK, GLA_DV)
    return pl.pallas_call(
        functools.partial(_gla_body, chunk=chunk, n_sub=n_sub, valid=valid),
        grid=(bsz, tlen // tb),
        in_specs=[seq(GLA_QK_W), seq(GLA_QK_W), seq(GLA_QK_W), seq(GLA_WIDTH), pl.BlockSpec(st, s0_map)],
        out_specs=[seq(GLA_WIDTH), pl.BlockSpec(st, lambda b, t: (b, 0, 0, 0))],
        out_shape=[jax.ShapeDtypeStruct((bsz, tlen, GLA_WIDTH), F32),
                   jax.ShapeDtypeStruct((bsz, N_HEADS, GLA_DK, GLA_DV), F32)],
        scratch_shapes=[pltpu.VMEM((GLA_QK_W, GLA_WIDTH), F32)],
        compiler_params=_cparams(("arbitrary", "arbitrary")),
        name="gla_scan",
    )(gq, gk, glog, gv, s0)


def _attn_body(lam_ref, bound_ref, q_ref, km_ref, vm_ref, k_ref, v_ref, o_ref, l_ref, acc_ref, *, tq):
    i = pl.program_id(2)
    q = q_ref[0]
    lane = lax.broadcasted_iota(I32, (tq, DIFF_DV), 1)
    zero = jnp.zeros_like(q)
    qs = jnp.concatenate([jnp.where(lane < DIFF_DQK, q, zero), jnp.where(lane >= DIFF_DQK, q, zero)], axis=0)
    diag_mask = lax.broadcasted_iota(I32, (2 * tq, tq), 1) <= _iota_mod((2 * tq, tq), 0, tq)
    bound = bound_ref[0]
    n_meta = km_ref.shape[0]

    def keys(j):
        return k_ref[0, pl.ds(pl.multiple_of(j * tq, tq), tq), :]

    def values(j):
        return v_ref[0, pl.ds(pl.multiple_of(j * tq, tq), tq), :]

    def finish(acc, l):
        o = acc / l
        o_ref[0] = o[:tq] - lam_ref[0] * o[tq:]

    @pl.when(bound <= SAFE_SCORE_BOUND)
    def _():
        def fold(p):
            return sum(p[:, V7X_LANES * c:V7X_LANES * (c + 1)] for c in range(tq // V7X_LANES))

        l_ref[...] = jnp.zeros_like(l_ref)
        acc_ref[...] = jnp.zeros_like(acc_ref)

        def full_chunk(j, carry):
            p = jnp.exp(_dot_nt(qs, keys(j)) - bound)
            l_ref[...] += fold(p)
            acc_ref[...] += _dot(p.astype(BF16), values(j))
            return carry

        lax.fori_loop(0, i, full_chunk, 0)
        p = jnp.where(diag_mask, jnp.exp(_dot_nt(qs, keys(i)) - bound), 0.0)
        pm = jnp.exp(_dot_nt(qs, km_ref[...]) - bound)
        acc = acc_ref[...] + _dot(p.astype(BF16), values(i)) + _dot(pm.astype(BF16), vm_ref[...])
        l = jnp.sum(l_ref[...] + fold(p), axis=-1, keepdims=True) + jnp.sum(pm, axis=-1, keepdims=True)
        finish(acc, l)

    @pl.when(bound > SAFE_SCORE_BOUND)
    def _():
        def step(k, v, carry, mask):
            m, l, acc = carry
            s = _dot_nt(qs, k)
            if mask is not None:
                s = jnp.where(mask, s, NEG_BIG)
            m_new = jnp.maximum(m, jnp.max(s, axis=-1, keepdims=True))
            alpha = jnp.exp(m - m_new)
            p = jnp.exp(s - m_new)
            l = alpha * l + jnp.sum(p, axis=-1, keepdims=True)
            acc = alpha * acc + _dot(p.astype(BF16), v)
            return m_new, l, acc

        carry = (jnp.full((2 * tq, 1), NEG_BIG, F32), jnp.zeros((2 * tq, 1), F32),
                 jnp.zeros((2 * tq, DIFF_DV), F32))
        carry = step(km_ref[...], vm_ref[...], carry, None)
        carry = lax.fori_loop(0, i, lambda j, c: step(keys(j), values(j), c, None), carry)
        _, l, acc = step(keys(i), values(i), carry, diag_mask)
        finish(acc, l)


def _attn_prompt(lam, bound, dq, km, vm, kb, vb, *, tq):
    bsz, tlen, _ = dq.shape
    assert tlen % tq == 0 and tq % V7X_LANES == 0
    meta_spec = pl.BlockSpec((km.shape[0], DIFF_DV), lambda b, h, i: (0, h))
    kv_spec = pl.BlockSpec((1, tlen, DIFF_DV), lambda b, h, i: (b, 0, h))
    q_spec = pl.BlockSpec((1, tq, DIFF_DV), lambda b, h, i: (b, i, h))
    smem = pl.BlockSpec(memory_space=pltpu.SMEM)
    return pl.pallas_call(
        functools.partial(_attn_body, tq=tq),
        grid=(bsz, N_HEADS, tlen // tq),
        in_specs=[smem, smem, q_spec, meta_spec, meta_spec, kv_spec, kv_spec],
        out_specs=q_spec,
        out_shape=jax.ShapeDtypeStruct((bsz, tlen, DIFF_W), F32),
        scratch_shapes=[pltpu.VMEM((2 * tq, V7X_LANES), F32), pltpu.VMEM((2 * tq, DIFF_DV), F32)],
        compiler_params=_cparams(("arbitrary", "arbitrary", "arbitrary")),
        name="diff_attn_prompt",
    )(lam, bound, dq, km, vm, kb, vb)


def _attn_paged_body(pt_ref, lam_ref, q_ref, kn_ref, vn_ref, *rest, n_pages, qrows, tnew):
    del pt_ref
    k_refs = rest[:n_pages]
    v_refs = rest[n_pages:2 * n_pages]
    o_ref, m_ref, l_ref, acc_ref = rest[2 * n_pages:]
    g = pl.program_id(1)
    n_rows = 2 * N_HEADS * qrows
    tpad = kn_ref.shape[1]
    q = q_ref[0]
    lane = lax.broadcasted_iota(I32, (qrows, DIFF_DV), 1)
    blocks = []
    for h in range(N_HEADS):
        qh = q[0:qrows, DIFF_DV * h:DIFF_DV * (h + 1)]
        blocks += [jnp.where(lane < DIFF_DQK, qh, 0.0), jnp.where(lane >= DIFF_DQK, qh, 0.0)]
    qr = jnp.concatenate(blocks, axis=0).astype(BF16)

    def update(s, v):
        m = m_ref[...]
        m_new = jnp.maximum(m, jnp.max(s, axis=-1, keepdims=True))
        alpha = jnp.exp(m - m_new)
        p = jnp.exp(s - m_new)
        l_ref[...] = alpha * l_ref[...] + jnp.sum(p, axis=-1, keepdims=True)
        acc_ref[...] = alpha * acc_ref[...] + _dot(p.astype(BF16), v)
        m_ref[...] = m_new

    @pl.when(g == 0)
    def _():
        m_ref[...] = jnp.full_like(m_ref, NEG_BIG)
        l_ref[...] = jnp.zeros_like(l_ref)
        acc_ref[...] = jnp.zeros_like(acc_ref)
        kn = jnp.concatenate([kn_ref[0, :, DIFF_DV * h:DIFF_DV * (h + 1)] for h in range(N_HEADS)], axis=0)
        vn = jnp.concatenate([vn_ref[0, :, DIFF_DV * h:DIFF_DV * (h + 1)] for h in range(N_HEADS)], axis=0)
        shape = (n_rows, N_HEADS * tpad)
        same_head = _iota_div(shape, 0, 2 * qrows) == _iota_div(shape, 1, tpad)
        tok = _iota_mod(shape, 1, tpad)
        visible = same_head & (tok <= _iota_mod(shape, 0, qrows)) & (tok < tnew)
        update(jnp.where(visible, _dot_nt(qr, kn), NEG_BIG), vn)

    k = jnp.concatenate([r[...].astype(BF16) for r in k_refs], axis=0)
    v = jnp.concatenate([r[...].astype(BF16) for r in v_refs], axis=0)
    tile = (n_rows, V7X_LANES)
    head_bias = jnp.where(_iota_mod(tile, 1, N_HEADS) == _iota_div(tile, 0, 2 * qrows), 0.0, NEG_BIG)
    update(_dot_nt(qr, k) + jnp.concatenate([head_bias] * (k.shape[0] // V7X_LANES), axis=1), v)

    @pl.when(g == pl.num_programs(1) - 1)
    def _():
        o = acc_ref[...] / l_ref[...]
        lam = lam_ref[0]
        for h in range(N_HEADS):
            r0 = 2 * qrows * h
            o_ref[0, :, DIFF_DV * h:DIFF_DV * (h + 1)] = o[r0:r0 + qrows] - lam * o[r0 + qrows:r0 + 2 * qrows]


def _attn_paged(page_table, lam, dq, kn, vn, cache_k, cache_v, *, page, tnew):
    dbsz, tpad, _ = dq.shape
    n_tbl = page_table.shape[1]
    qrows = V7X_SUBLANES
    assert tnew <= qrows <= tpad
    n_pages = PAGES_PER_STEP
    while n_tbl % n_pages:
        n_pages //= 2
    n_rows = 2 * N_HEADS * qrows
    tok_spec = pl.BlockSpec((1, tpad, DIFF_W), lambda b, g, pt: (b, 0, 0))
    out_spec = pl.BlockSpec((1, qrows, DIFF_W), lambda b, g, pt: (b, 0, 0))

    def page_spec(j):
        return pl.BlockSpec((page * N_HEADS, DIFF_DV), lambda b, g, pt: (pt[b, g * n_pages + j], 0))

    grid_spec = pltpu.PrefetchScalarGridSpec(
        num_scalar_prefetch=1,
        grid=(dbsz, n_tbl // n_pages),
        in_specs=[pl.BlockSpec(memory_space=pltpu.SMEM), tok_spec, tok_spec, tok_spec]
        + [page_spec(j) for j in range(n_pages)] * 2,
        out_specs=out_spec,
        scratch_shapes=[pltpu.VMEM((n_rows, 1), F32), pltpu.VMEM((n_rows, 1), F32),
                        pltpu.VMEM((n_rows, DIFF_DV), F32)],
    )
    return pl.pallas_call(
        functools.partial(_attn_paged_body, n_pages=n_pages, qrows=qrows, tnew=tnew),
        grid_spec=grid_spec,
        out_shape=jax.ShapeDtypeStruct((dbsz, qrows, DIFF_W), F32),
        compiler_params=_cparams(("arbitrary", "arbitrary")),
        name="diff_attn_paged",
    )(page_table, lam, dq, kn, vn, *([cache_k] * n_pages), *([cache_v] * n_pages))


N_MIX_INPUTS = 11


def _mix_body(*refs, n_blocks):
    ins, outs = refs[:N_MIX_INPUTS], refs[-4:]
    i = pl.program_id(0)

    @pl.when(i < n_blocks)
    def _():
        _mix_tokens(*ins, *outs)

    @pl.when(i >= n_blocks)
    def _():
        for r in outs:
            r[...] = jnp.zeros_like(r)


def _mix_tokens(og_ref, gg_ref, od_ref, x_ref, ggain_ref, dgain_ref, wo_ref, n2_ref, wrh_ref, wrl_ref, br_ref,
                h_ref, xn_ref, ei_ref, gt_ref):
    def head_norm(z, gain):
        parts = []
        for h in range(N_HEADS):
            seg = z[:, GLA_DV * h:GLA_DV * (h + 1)]
            parts.append(seg * lax.rsqrt(jnp.mean(seg * seg, axis=-1, keepdims=True) + EPS))
        return jnp.concatenate(parts, axis=-1) * gain

    gg = gg_ref[...]
    a = head_norm(og_ref[...], ggain_ref[...]) * (gg / (1.0 + jnp.exp(-gg)))
    d = head_norm(od_ref[...], dgain_ref[...]) * (1.0 - LAM_INIT)
    cat = jnp.concatenate([a, d], axis=-1).astype(BF16)
    hres = x_ref[...] + _dot(cat, wo_ref[...])
    h_ref[...] = hres

    xn = hres * lax.rsqrt(jnp.mean(hres * hres, axis=-1, keepdims=True) + EPS) * n2_ref[...]
    _store_row_tiles(xn_ref, xn)

    x_hi, x_lo = _split_bf16(xn)
    logits = _dot(x_hi, wrh_ref[...]) + _dot(x_lo, wrh_ref[...]) + _dot(x_hi, wrl_ref[...]) + br_ref[...]
    lane = lax.broadcasted_iota(I32, logits.shape, 1).astype(F32)
    far = jnp.float32(1e4)

    def rmax(z):
        return jnp.max(z, axis=-1, keepdims=True)

    def first_lane(hit):
        return jnp.min(jnp.where(hit, lane, far), axis=-1, keepdims=True)

    is_group = lane < N_GROUPS
    lg = jnp.where(is_group, logits, NEG_BIG)
    mg = rmax(lg)
    grp = first_lane(lg == mg)
    g_gate = 1.0 / jnp.sum(jnp.where(is_group, jnp.exp(lg - mg), 0.0), axis=-1, keepdims=True)
    lo = N_GROUPS + EXPERTS_PER_GROUP * grp
    in_grp = (lane >= lo) & (lane < lo + EXPERTS_PER_GROUP)
    le = jnp.where(in_grp, logits, NEG_BIG)
    v1 = rmax(le)
    i1 = first_lane(in_grp & (le == v1))
    rest = in_grp & (lane != i1)
    le2 = jnp.where(rest, logits, NEG_BIG)
    v2 = rmax(le2)
    i2 = first_lane(rest & (le2 == v2))
    e21 = jnp.exp(v2 - v1)
    p1 = 1.0 / (1.0 + e21)
    ei = jnp.where(lane == 0, i1 - N_GROUPS, jnp.where(lane == 1, i2 - N_GROUPS, 0.0))
    ei_ref[...] = ei.astype(I32)
    gt_ref[...] = jnp.where(lane == 0, g_gate * p1, jnp.where(lane == 1, g_gate * (e21 * p1), 0.0))


def _mix_out(o_gla, gg, o_diff, x2d, prep, tile, n_total, row_start=0, base=None):
    n = x2d.shape[0]
    tm = _row_tile(n, tile)
    assert row_start % tm == 0 and n_total % tm == 0
    off = row_start // tm
    n_blocks = n // tm
    n_steps = n_blocks if base is not None else n_total // tm - off
    row = lambda w: pl.BlockSpec((tm, w), lambda i: (jnp.minimum(i, n_blocks - 1), 0))
    out_row = lambda w: pl.BlockSpec((tm, w), lambda i: (i + off, 0))
    full = lambda a: pl.BlockSpec(a.shape, lambda i: (0,) * a.ndim)
    consts = (prep["gla_gain"], prep["diff_gain"], prep["w_out"], prep["norm2"], prep["w_route_hi"],
              prep["w_route_lo"], prep["b_route"])
    base = () if base is None else tuple(base)
    n_in = 4 + len(consts)
    assert n_in == N_MIX_INPUTS
    return pl.pallas_call(
        functools.partial(_mix_body, n_blocks=n_blocks),
        grid=(n_steps,),
        in_specs=[row(GLA_WIDTH), row(GLA_WIDTH), row(DIFF_W), row(D_MODEL)] + [full(c) for c in consts]
        + [pl.BlockSpec(memory_space=pl.ANY)] * len(base),
        out_specs=[out_row(D_MODEL), pl.BlockSpec((tm * ROW_TILE, V7X_LANES), lambda i: (i + off, 0)),
                   out_row(V7X_LANES), out_row(V7X_LANES)],
        out_shape=[jax.ShapeDtypeStruct((n_total, D_MODEL), F32),
                   jax.ShapeDtypeStruct((n_total * ROW_TILE, V7X_LANES), F32),
                   jax.ShapeDtypeStruct((n_total, V7X_LANES), I32), jax.ShapeDtypeStruct((n_total, V7X_LANES), F32)],
        input_output_aliases={n_in + j: j for j in range(len(base))},
        compiler_params=_cparams(("arbitrary",)),
        name="mix_out_route",
    )(o_gla, gg, o_diff, x2d, *consts, *base)


def _rank_body(ei_ref, rank_ref, cnt_ref, carry_ref):
    i = pl.program_id(0)

    @pl.when(i == 0)
    def _():
        carry_ref[...] = jnp.zeros_like(carry_ref)

    ei = ei_ref[...]
    tb = ei.shape[0]
    lane = lax.broadcasted_iota(I32, ei.shape, 1)
    e0 = ei[:, 0:1]
    e1 = ei[:, 1:2]
    oh0 = lane == e0
    oh1 = lane == e1
    cnt = oh0.astype(F32) + oh1.astype(F32)
    ri = lax.broadcasted_iota(I32, (tb, tb), 0)
    ci = lax.broadcasted_iota(I32, (tb, tb), 1)
    before = _dot((ci < ri).astype(BF16), cnt.astype(BF16)) + carry_ref[0:1, :]
    r0 = jnp.sum(jnp.where(oh0, before, 0.0), axis=-1, keepdims=True)
    r1 = jnp.sum(jnp.where(oh1, before, 0.0), axis=-1, keepdims=True)
    rank_ref[...] = jnp.where(lane == 0, r0, jnp.where(lane == 1, r1, 0.0)).astype(I32)
    total = carry_ref[0:1, :] + jnp.sum(cnt, axis=0, keepdims=True)
    carry_ref[...] = jnp.broadcast_to(total, carry_ref.shape)
    cnt_ref[...] = jnp.broadcast_to(total, cnt_ref.shape).astype(I32)


def _rank(ei):
    n = ei.shape[0]
    tb = _row_tile(n, 256)
    row = pl.BlockSpec((tb, V7X_LANES), lambda i: (i, 0))
    one = pl.BlockSpec((V7X_SUBLANES, V7X_LANES), lambda i: (0, 0))
    return pl.pallas_call(
        _rank_body,
        grid=(n // tb,),
        in_specs=[row],
        out_specs=[row, one],
        out_shape=[jax.ShapeDtypeStruct((n, V7X_LANES), I32), jax.ShapeDtypeStruct((V7X_SUBLANES, V7X_LANES), I32)],
        scratch_shapes=[pltpu.VMEM((V7X_SUBLANES, V7X_LANES), F32)],
        compiler_params=_cparams(("arbitrary",)),
        name="moe_rank",
    )(ei)


def _tile_rows(ref, row):
    return ref.at[pl.ds(pl.multiple_of(row * ROW_TILE, ROW_TILE), ROW_TILE)]


def _issue_loop(n, per_item):
    assert n % DMA_ISSUE_UNROLL == 0

    def group(g, c):
        for u in range(DMA_ISSUE_UNROLL):
            per_item(g * DMA_ISSUE_UNROLL + u, u)
        return c

    lax.fori_loop(0, n // DMA_ISSUE_UNROLL, group, 0)


def _dispatch_body(dest_ref, zfill_ref, x_ref, xb_ref, zero_ref, sem, zsem, *, tb):
    i = pl.program_id(0)

    def zero_copy(e):
        start = pl.multiple_of(zfill_ref[e] * ROW_TILE, ROW_TILE)
        return pltpu.make_async_copy(zero_ref, xb_ref.at[pl.ds(start, MOE_BLOCK * ROW_TILE)], zsem)

    @pl.when(i == 0)
    def _():
        zero_ref[...] = jnp.zeros_like(zero_ref)
        for e in range(zfill_ref.shape[0]):
            @pl.when(zfill_ref[e] >= 0)
            def _():
                zero_copy(e).start()
        for e in range(zfill_ref.shape[0]):
            @pl.when(zfill_ref[e] >= 0)
            def _():
                zero_copy(e).wait()

    def copy(t, k):
        return pltpu.make_async_copy(_tile_rows(x_ref, t), _tile_rows(xb_ref, dest_ref[0, 0, 2 * t + k]), sem)

    def start(t, u):
        copy(t, 0).start(priority=0)
        copy(t, 1).start(priority=1)

    def wait(t, u):
        copy(t, 0).wait()
        copy(t, 1).wait()

    _issue_loop(tb, start)
    _issue_loop(tb, wait)


def _dispatch(dest, zfill, xn_tiles, cap):
    n = xn_tiles.shape[0] // ROW_TILE
    tb = _row_tile(n, 512)
    return pl.pallas_call(
        functools.partial(_dispatch_body, tb=tb),
        grid=(n // tb,),
        in_specs=[pl.BlockSpec((1, 1, 2 * tb), lambda i: (i, 0, 0), memory_space=pltpu.SMEM),
                  pl.BlockSpec(memory_space=pltpu.SMEM),
                  pl.BlockSpec((tb * ROW_TILE, V7X_LANES), lambda i: (i, 0))],
        out_specs=pl.BlockSpec(memory_space=pl.ANY),
        out_shape=jax.ShapeDtypeStruct((cap * ROW_TILE, V7X_LANES), F32),
        scratch_shapes=[pltpu.VMEM((MOE_BLOCK * ROW_TILE, V7X_LANES), F32), pltpu.SemaphoreType.DMA(()),
                        pltpu.SemaphoreType.DMA(())],
        compiler_params=_cparams(("arbitrary",)),
        name="moe_dispatch",
    )(dest.reshape(n // tb, 1, 2 * tb), zfill, xn_tiles)


def _expert_body(be_ref, bv_ref, xb_ref, w1_ref, w3_ref, w2_ref, yb_ref, w1b, w3b, w2b):
    i = pl.program_id(0)
    prev = be_ref[jnp.maximum(i - 1, 0)]

    @pl.when((i == 0) | (be_ref[i] != prev))
    def _():
        w1b[...] = w1_ref[0].astype(BF16)
        w3b[...] = w3_ref[0].astype(BF16)
        w2b[...] = w2_ref[0].astype(BF16)

    @pl.when(bv_ref[i] > 0)
    def _():
        x = _load_row_tiles(xb_ref, MOE_BLOCK).astype(BF16)
        up = _dot(x, w1b[...])
        hid = (up / (1.0 + jnp.exp(-up))) * _dot(x, w3b[...])
        _store_row_tiles(yb_ref, _dot(hid.astype(BF16), w2b[...]))

    @pl.when(bv_ref[i] == 0)
    def _():
        yb_ref[...] = jnp.zeros_like(yb_ref)


def _experts(blk_expert, blk_valid, xb, w_up, w_gate, w_down):
    cap = xb.shape[0] // ROW_TILE
    n_blk = cap // MOE_BLOCK
    w13 = pl.BlockSpec((1, D_MODEL, D_EXPERT), lambda i, be, bv: (be[i], 0, 0))
    w2 = pl.BlockSpec((1, D_EXPERT, D_MODEL), lambda i, be, bv: (be[i], 0, 0))
    rows = pl.BlockSpec((MOE_BLOCK * ROW_TILE, V7X_LANES), lambda i, be, bv: (i, 0))
    grid_spec = pltpu.PrefetchScalarGridSpec(
        num_scalar_prefetch=2,
        grid=(n_blk,),
        in_specs=[rows, w13, w13, w2],
        out_specs=rows,
        scratch_shapes=[pltpu.VMEM((D_MODEL, D_EXPERT), BF16), pltpu.VMEM((D_MODEL, D_EXPERT), BF16),
                        pltpu.VMEM((D_EXPERT, D_MODEL), BF16)],
    )
    return pl.pallas_call(
        _expert_body,
        grid_spec=grid_spec,
        out_shape=jax.ShapeDtypeStruct((cap * ROW_TILE, V7X_LANES), F32),
        compiler_params=_cparams(("arbitrary",)),
        name="moe_experts",
    )(blk_expert, blk_valid, xb, w_up, w_gate, w_down)


def _combine_body(dest_ref, next_ref, h_ref, gt_ref, yb_ref, o_ref, buf, sem, *, tb):
    i = pl.program_id(0)
    slot = i & 1

    def copy(idx_ref, s, t, k):
        return pltpu.make_async_copy(_tile_rows(yb_ref, idx_ref[0, 0, 2 * t + k]), _tile_rows(buf.at[s, k], t),
                                     sem.at[s])

    def issue(idx_ref, s):
        def start(t, u):
            copy(idx_ref, s, t, 0).start(priority=0)
            copy(idx_ref, s, t, 1).start(priority=1)

        _issue_loop(tb, start)

    @pl.when(i == 0)
    def _():
        issue(dest_ref, 0)

    @pl.when(i + 1 < pl.num_programs(0))
    def _():
        issue(next_ref, 1 - slot)

    def wait(t, u):
        copy(dest_ref, slot, t, 0).wait()
        copy(dest_ref, slot, t, 1).wait()

    _issue_loop(tb, wait)
    gt = gt_ref[...]
    y0 = _load_row_tiles(buf.at[slot, 0], tb)
    y1 = _load_row_tiles(buf.at[slot, 1], tb)
    o_ref[...] = h_ref[...] + gt[:, 0:1] * y0 + gt[:, 1:2] * y1


def _combine(dest, hres, gt, yb, row_start, n_rows):
    n_total = hres.shape[0]
    tb = _row_tile(n_rows, 256)
    assert row_start % tb == 0 and n_total % tb == 0
    off = row_start // tb
    n_steps = n_rows // tb
    idx = lambda f: pl.BlockSpec((1, 1, 2 * tb), lambda i: (f(i) + off, 0, 0), memory_space=pltpu.SMEM)
    return pl.pallas_call(
        functools.partial(_combine_body, tb=tb),
        grid=(n_steps,),
        in_specs=[idx(lambda i: i), idx(lambda i: jnp.minimum(i + 1, n_steps - 1)),
                  pl.BlockSpec((tb, D_MODEL), lambda i: (i + off, 0)),
                  pl.BlockSpec((tb, V7X_LANES), lambda i: (i + off, 0)),
                  pl.BlockSpec(memory_space=pl.ANY)],
        out_specs=pl.BlockSpec((tb, D_MODEL), lambda i: (i, 0)),
        out_shape=jax.ShapeDtypeStruct((n_rows, D_MODEL), F32),
        scratch_shapes=[pltpu.VMEM((2, 2, tb * ROW_TILE, V7X_LANES), F32), pltpu.SemaphoreType.DMA((2,))],
        compiler_params=_cparams(("arbitrary",)),
        name="moe_combine",
    )(dest.reshape(n_total // tb, 1, 2 * tb), dest.reshape(n_total // tb, 1, 2 * tb), hres, gt, yb)


def _moe(hres, xn, ei, gt, w_up, w_gate, w_down, splits):
    n = hres.shape[0]
    rank, counts = _rank(ei)
    counts = counts[0, :N_EXPERTS]
    padded = (counts + MOE_BLOCK - 1) // MOE_BLOCK * MOE_BLOCK
    pad_end = jnp.cumsum(padded)
    pad_start = pad_end - padded
    e2 = ei[:, :2]
    dest = (pad_start[e2] + rank[:, :2]).reshape(-1)
    n_blk = -(-(2 * n + N_EXPERTS * (MOE_BLOCK - 1)) // MOE_BLOCK)
    blk_start = jnp.arange(n_blk, dtype=I32) * MOE_BLOCK
    blk_expert = jnp.sum((pad_end[None, :] <= blk_start[:, None]).astype(I32), axis=1)
    blk_expert = jnp.minimum(blk_expert, N_EXPERTS - 1)
    blk_valid = (blk_start < (pad_start + counts)[blk_expert]).astype(I32)
    last_used = jnp.max(jnp.where(blk_valid > 0, blk_expert, 0))
    blk_expert = jnp.where(blk_start < pad_end[-1], blk_expert, last_used)
    tail_start = blk_start[n_blk - N_EXPERTS:]
    zfill = jnp.concatenate([jnp.where(padded > 0, pad_end - MOE_BLOCK, -1),
                             jnp.where(tail_start >= pad_end[-1], tail_start, -1)]).astype(I32)
    xb = _dispatch(dest, zfill, xn, n_blk * MOE_BLOCK)
    yb = _experts(blk_expert, blk_valid, xb, w_up, w_gate, w_down)
    return [_combine(dest, hres, gt, yb, start, rows) for start, rows in splits]


def _prepare(norm1, w_in, w_gk2, b_gk2, gla_norm, q_norm, k_norm, diff_norm, w_out, norm2,
             w_route_group, b_route_group, w_route_expert, b_route_expert):
    gq, gk, gv, gg, glr, dq, dk, dv = jnp.split(
        w_in, [256, 512, 1024, 1536, 1552, 2064, 2576], axis=-1)
    glr = jnp.pad(glr, ((0, 0), (0, V7X_LANES - GLA_RANK)))
    w_packed = jnp.concatenate([gq, gk, gv, gg, dq, dk, dv, glr], axis=-1).astype(BF16)
    w_route = jnp.concatenate([w_route_group, w_route_expert], axis=-1)
    w_route = jnp.pad(w_route, ((0, 0), (0, V7X_LANES - N_GROUPS - N_EXPERTS)))
    w_route_hi = w_route.astype(BF16)
    b_route = jnp.pad(jnp.concatenate([b_route_group, b_route_expert]), (0, V7X_LANES - N_GROUPS - N_EXPERTS))
    gidx = jnp.arange(256) // DIFF_DQK
    return {
        "norm1": norm1.reshape(1, D_MODEL),
        "w_in": w_packed,
        "w_gk2": jnp.pad(w_gk2, ((0, V7X_LANES - GLA_RANK), (0, 0))).astype(BF16),
        "b_gk2": b_gk2.reshape(1, GLA_QK_W),
        "q_gain": jnp.tile(q_norm.reshape(-1), N_HEADS).reshape(1, DIFF_W),
        "k_gain": jnp.tile(k_norm.reshape(-1), N_HEADS).reshape(1, DIFF_W),
        "group_ones": (gidx[:, None] == gidx[None, :]).astype(BF16),
        "gla_gain": jnp.tile(gla_norm, N_HEADS).reshape(1, GLA_WIDTH),
        "diff_gain": jnp.tile(diff_norm, N_HEADS).reshape(1, DIFF_W),
        "w_out": w_out.astype(BF16),
        "norm2": norm2.reshape(1, D_MODEL),
        "w_route_hi": w_route_hi,
        "w_route_lo": (w_route - w_route_hi.astype(F32)).astype(BF16),
        "b_route": b_route.reshape(1, V7X_LANES),
    }


def kernel(x_prompt, x_sample, cache_k, cache_v, state_gla, page_table, meta_tokens, norm1, w_in, w_gk2, b_gk2,
           gla_norm, q_norm, k_norm, lam_q1, lam_k1, lam_q2, lam_k2, diff_norm, w_out, norm2, w_route_group,
           b_route_group, w_route_expert, b_route_expert, w_up, w_gate, w_down):
    bsz, seq, _ = x_prompt.shape
    dbsz, dseq, _ = x_sample.shape
    n_phys, page = cache_k.shape[:2]
    prep = _prepare(norm1[0], w_in[0], w_gk2[0], b_gk2[0], gla_norm[0], q_norm[0], k_norm[0], diff_norm[0],
                    w_out[0], norm2[0], w_route_group[0], b_route_group[0], w_route_expert[0], b_route_expert[0])
    lam = (jnp.exp(jnp.sum(lam_q1[0] * lam_k1[0])) - jnp.exp(jnp.sum(lam_q2[0] * lam_k2[0])) + LAM_INIT).reshape(1)
    w_up, w_gate, w_down = w_up[0], w_gate[0], w_down[0]

    mq, mk, mv, _, mglog, _, mdk, mdv, mdkb, mdvb = _inproj(meta_tokens, prep, N_META)
    s_zero = jnp.zeros((1, N_HEADS, GLA_DK, GLA_DV), F32)
    _, s_meta = _gla(mq[None], mk[None], mglog[None], mv[None], s_zero, chunk=N_META, n_sub=1)

    xp = x_prompt.reshape(bsz * seq, D_MODEL)
    gq, gk, gv, gg, glog, dq, dk, dv, dkb, dvb = _inproj(xp, prep, 512)
    seq3 = lambda a: a.reshape(bsz, seq, a.shape[-1])
    n_sub = 4 if seq % (4 * GLA_CHUNK) == 0 else 1
    o_gla, gla_prompt = _gla(seq3(gq), seq3(gk), seq3(glog), seq3(gv), s_meta, chunk=GLA_CHUNK, n_sub=n_sub)

    def with_meta(meta_rows, rows):
        meta_b = jnp.broadcast_to(meta_rows[None], (bsz,) + meta_rows.shape)
        return jnp.concatenate([meta_b, seq3(rows)], axis=1)

    k_prompt = with_meta(mdk, dk)
    v_prompt = with_meta(mdv, dv)
    bound = (8.1 * jnp.max(jnp.abs(q_norm[0])) * jnp.max(jnp.abs(k_norm[0]))).reshape(1)
    o_diff = _attn_prompt(lam, bound, seq3(dq), mdkb, mdvb, seq3(dkb), seq3(dvb), tq=_row_tile(seq, ATTN_TQ))
    n_p, n_s = bsz * seq, dbsz * dseq
    n_s_pad = -(-n_s // MOE_TOKEN_TILE) * MOE_TOKEN_TILE
    n_tok = n_p + n_s_pad
    tokens = _mix_out(o_gla.reshape(n_p, GLA_WIDTH), gg, o_diff.reshape(n_p, DIFF_W), xp, prep, MOE_TOKEN_TILE,
                      n_tok)

    tpad = -(-dseq // V7X_BF16_SUBLANES) * V7X_BF16_SUBLANES
    xs = jnp.pad(x_sample, ((0, 0), (0, tpad - dseq), (0, 0))).reshape(dbsz * tpad, D_MODEL)
    sq, sk, sv, sg, sglog, sdq, sdk, sdv, sdkb, sdvb = _inproj(xs, prep, 128)
    pad3 = lambda a: a.reshape(dbsz, tpad, a.shape[-1])
    unpad = lambda a: pad3(a)[:, :dseq]
    rows = lambda a: a.reshape(dbsz * dseq, a.shape[-1])
    o_gla_s, gla_sample = _gla(pad3(sq), pad3(sk), pad3(sglog), pad3(sv), state_gla[:, 0], chunk=tpad, n_sub=1,
                               valid=dseq)
    cache_rows = lambda c: c.reshape(n_phys * page * N_HEADS, DIFF_DV)
    o_diff_s = _attn_paged(page_table, lam, pad3(sdq).astype(F32), pad3(sdkb), pad3(sdvb), cache_rows(cache_k),
                           cache_rows(cache_v), page=page, tnew=dseq)
    tile_pad = lambda a: jnp.pad(rows(a), ((0, n_s_pad - n_s), (0, 0)))
    hres, xn_tiles, ei, gt = _mix_out(tile_pad(unpad(o_gla_s)), tile_pad(unpad(sg)), tile_pad(o_diff_s[:, :dseq]),
                                      tile_pad(x_sample), prep, MOE_TOKEN_TILE, n_tok, row_start=n_p, base=tokens)
    y_prompt, y_sample = _moe(hres, xn_tiles, ei, gt, w_up, w_gate, w_down, [(0, n_p), (n_p, n_s_pad)])
    y_prompt = y_prompt.reshape(bsz, seq, D_MODEL)
    y_sample = y_sample[:n_s].reshape(dbsz, dseq, D_MODEL)

    heads = lambda a: a.reshape(a.shape[0], a.shape[1], 1, N_HEADS, DIFF_DV)
    return (y_prompt, y_sample, heads(k_prompt), heads(v_prompt), gla_prompt[:, None],
            heads(unpad(sdk)), heads(unpad(sdv)), gla_sample[:, None])
```

```python
import functools

import jax
import jax.numpy as jnp
from jax import lax
from jax.experimental import pallas as pl
from jax.experimental.pallas import tpu as pltpu

F32 = jnp.float32
BF16 = jnp.bfloat16
I32 = jnp.int32

V7X_LANES = 128
V7X_SUBLANES = 8
V7X_BF16_SUBLANES = 16
V7X_VMEM_LIMIT_BYTES = 56 * 1024 * 1024

D_MODEL = 1024
N_META = 16
N_HEADS = 4
GLA_DK = 64
GLA_DV = 128
GLA_RANK = 16
GLA_TAU = 16.0
GLA_QK_W = N_HEADS * GLA_DK
GLA_WIDTH = N_HEADS * GLA_DV
DIFF_DQK = 64
DIFF_DV = 128
DIFF_W = N_HEADS * DIFF_DV
N_GROUPS = 4
EXPERTS_PER_GROUP = 8
N_EXPERTS = N_GROUPS * EXPERTS_PER_GROUP
D_EXPERT = D_MODEL // 2
EPS = 1e-6
LAM_INIT = 0.2
NEG_BIG = -1e30
EXP_CLAMP = 80.0

OFF_GQ, OFF_GK, OFF_GV, OFF_GG = 0, 256, 512, 1024
OFF_DQ, OFF_DK, OFF_DV, OFF_GLR = 1536, 2048, 2560, 3072
PACKED_IN_W = OFF_GLR + V7X_LANES

GLA_CHUNK = 64
GLA_SUBCHUNKS = 8
ATTN_TQ = 512
MOE_BLOCK = 256
MOE_TOKEN_TILE = 512
PAGES_PER_STEP = 32
PAGE_GROUP = 16
DMA_ISSUE_UNROLL = 8
SAFE_SCORE_BOUND = 40.0


def _cparams(sem):
    return pltpu.CompilerParams(dimension_semantics=sem, vmem_limit_bytes=V7X_VMEM_LIMIT_BYTES)


def _row_tile(n, pref):
    t = min(n, pref)
    while n % t:
        t //= 2
    return t


def _dot(a, b):
    return jnp.dot(a, b, preferred_element_type=F32)


def _dot_nt(a, b):
    return lax.dot_general(a, b, (((1,), (1,)), ((), ())), preferred_element_type=F32)


def _dot_tn(a, b):
    return lax.dot_general(a, b, (((0,), (0,)), ((), ())), preferred_element_type=F32)


def _iota_div(shape, dim, d):
    assert d & (d - 1) == 0
    return lax.shift_right_logical(lax.broadcasted_iota(I32, shape, dim), d.bit_length() - 1)


def _iota_mod(shape, dim, d):
    assert d & (d - 1) == 0
    return lax.broadcasted_iota(I32, shape, dim) & (d - 1)


ROW_TILE = D_MODEL // V7X_LANES


def _store_row_tiles(ref, x):
    rows = x.shape[0]
    for s in range(ROW_TILE):
        ref[pl.ds(s, rows, stride=ROW_TILE), :] = x[:, V7X_LANES * s:V7X_LANES * (s + 1)]


def _load_row_tiles(ref, rows):
    return jnp.concatenate([ref[pl.ds(s, rows, stride=ROW_TILE), :] for s in range(ROW_TILE)], axis=1)


def _split_bf16(x):
    hi = x.astype(BF16)
    lo = (x - hi.astype(F32)).astype(BF16)
    return hi, lo


def _inproj_body(*refs, seq_layout):
    x_ref, n1_ref, w_ref, wg2_ref, bg2_ref, qg_ref, kg_ref, gm_ref = refs[:8]
    if seq_layout is None:
        gq_ref, gk_ref, gv_ref, gg_ref, glog_ref, dq_ref, dkb_ref, dvb_ref, dk_ref, dv_ref = refs[8:]
    else:
        mk_ref, mv_ref = refs[8:10]
        gq_ref, gk_ref, gv_ref, gg_ref, glog_ref, dq_ref, dkb_ref, dvb_ref, kp_ref, vp_ref = refs[10:20]
        kbuf, vbuf, sem, msem = refs[20:]
    x = x_ref[...]
    ms = jnp.mean(x * x, axis=-1, keepdims=True)
    xn = (x * lax.rsqrt(ms + EPS) * n1_ref[...]).astype(BF16)

    def proj(off, width):
        return _dot(xn, w_ref[:, off:off + width])

    gq_ref[...] = proj(OFF_GQ, GLA_QK_W) * (GLA_DK ** -0.5)
    gk_ref[...] = proj(OFF_GK, GLA_QK_W)
    gv_ref[...] = proj(OFF_GV, GLA_WIDTH)
    gg_ref[...] = proj(OFF_GG, GLA_WIDTH)

    glr = proj(OFF_GLR, V7X_LANES).astype(BF16)
    gpre = _dot(glr, wg2_ref[...]) + bg2_ref[...]
    log_sig = jnp.minimum(gpre, 0.0) - jnp.log(1.0 + jnp.exp(-jnp.abs(gpre)))
    glog_ref[...] = log_sig * (1.0 / GLA_TAU)

    gm = gm_ref[...]

    def group_norm(z, gain):
        hi, lo = _split_bf16(z * z)
        parts = []
        for c in range(DIFF_W // 256):
            sl = slice(256 * c, 256 * (c + 1))
            parts.append(_dot(hi[:, sl], gm) + _dot(lo[:, sl], gm))
        ss = jnp.concatenate(parts, axis=-1)
        return z * lax.rsqrt(ss * (1.0 / DIFF_DQK) + EPS) * gain

    dq = group_norm(proj(OFF_DQ, DIFF_W), qg_ref[...])
    dq_ref[...] = (dq * (DIFF_DQK ** -0.5)).astype(BF16)
    dk = group_norm(proj(OFF_DK, DIFF_W), kg_ref[...])
    dkb_ref[...] = dk.astype(BF16)
    dv = proj(OFF_DV, DIFF_W)
    dvb_ref[...] = dv.astype(BF16)
    if seq_layout is None:
        dk_ref[...] = dk
        dv_ref[...] = dv
        return

    seq, n_meta = seq_layout
    tm = x.shape[0]
    tiles_per_seq = seq // tm
    g = pl.program_id(0)
    slot = g & 1
    b = lax.shift_right_logical(g, tiles_per_seq.bit_length() - 1)
    seq_row0 = b * ((seq + n_meta) * N_HEADS)
    row0 = pl.multiple_of(seq_row0 + (n_meta + (g & (tiles_per_seq - 1)) * tm) * N_HEADS, V7X_SUBLANES)

    def copies(s):
        dst = pl.ds(row0, tm * N_HEADS)
        return (pltpu.make_async_copy(kbuf.at[s], kp_ref.at[dst], sem.at[s]),
                pltpu.make_async_copy(vbuf.at[s], vp_ref.at[dst], sem.at[s]))

    def meta_copies():
        dst = pl.ds(pl.multiple_of(seq_row0, V7X_SUBLANES), n_meta * N_HEADS)
        return (pltpu.make_async_copy(mk_ref, kp_ref.at[dst], msem),
                pltpu.make_async_copy(mv_ref, vp_ref.at[dst], msem))

    def wait_slot(s):
        for c in copies(s):
            c.wait()

    @pl.when(g >= 2)
    def _():
        wait_slot(slot)

    @pl.when((g & (tiles_per_seq - 1)) == 0)
    def _():
        for c in meta_copies():
            c.start()

    for h in range(N_HEADS):
        cols = slice(DIFF_DV * h, DIFF_DV * (h + 1))
        kbuf.at[slot][pl.ds(h, tm, stride=N_HEADS), :] = dk[:, cols]
        vbuf.at[slot][pl.ds(h, tm, stride=N_HEADS), :] = dv[:, cols]

    @pl.when((g & (tiles_per_seq - 1)) == 0)
    def _():
        for c in meta_copies():
            c.wait()

    for c in copies(slot):
        c.start()

    last = pl.num_programs(0) - 1

    @pl.when((g == last) & (g >= 1))
    def _():
        wait_slot(1 - slot)

    @pl.when(g == last)
    def _():
        wait_slot(slot)


def _inproj(x2d, prep, tile, seq=None, meta_k=None, meta_v=None):
    n = x2d.shape[0]
    tm = _row_tile(n, tile)
    row = lambda w: pl.BlockSpec((tm, w), lambda i: (i, 0))
    full = lambda a: pl.BlockSpec(a.shape, lambda i: (0,) * a.ndim)
    consts = (prep["norm1"], prep["w_in"], prep["w_gk2"], prep["b_gk2"], prep["q_gain"], prep["k_gain"],
              prep["group_ones"])
    widths = (GLA_QK_W, GLA_QK_W, GLA_WIDTH, GLA_WIDTH, GLA_QK_W, DIFF_W, DIFF_W, DIFF_W)
    dtypes = (F32, F32, F32, F32, F32, BF16, BF16, BF16)
    out_specs = [row(w) for w in widths]
    out_shape = [jax.ShapeDtypeStruct((n, w), d) for w, d in zip(widths, dtypes)]
    if seq is None:
        seq_layout, extra, scratch = None, (), []
        out_specs += [row(DIFF_W)] * 2
        out_shape += [jax.ShapeDtypeStruct((n, DIFF_W), F32)] * 2
    else:
        n_meta = meta_k.shape[0] // N_HEADS
        tiles_per_seq = seq // tm
        assert seq % tm == 0 and n % seq == 0 and tiles_per_seq & (tiles_per_seq - 1) == 0
        seq_layout, extra = (seq, n_meta), (meta_k, meta_v)
        final_rows = (n // seq) * (seq + n_meta) * N_HEADS
        out_specs += [pl.BlockSpec(memory_space=pl.ANY)] * 2
        out_shape += [jax.ShapeDtypeStruct((final_rows, DIFF_DV), F32)] * 2
        scratch = [pltpu.VMEM((2, tm * N_HEADS, DIFF_DV), F32), pltpu.VMEM((2, tm * N_HEADS, DIFF_DV), F32),
                   pltpu.SemaphoreType.DMA((2,)), pltpu.SemaphoreType.DMA(())]
    return pl.pallas_call(
        functools.partial(_inproj_body, seq_layout=seq_layout),
        grid=(n // tm,),
        in_specs=[row(D_MODEL)] + [full(c) for c in consts + extra],
        out_specs=out_specs,
        out_shape=out_shape,
        scratch_shapes=scratch,
        compiler_params=_cparams(("arbitrary",)),
        name="inproj",
    )(x2d, *consts, *extra)


def _gla_body(q_ref, k_ref, g_ref, v_ref, s0_ref, o_ref, sfin_ref, state_ref, *, chunk, n_sub, valid):
    t = pl.program_id(1)
    c = chunk

    @pl.when(t == 0)
    def _():
        state_ref[...] = jnp.zeros_like(state_ref)
        for h in range(N_HEADS):
            state_ref[GLA_DK * h:GLA_DK * (h + 1), GLA_DV * h:GLA_DV * (h + 1)] = s0_ref[0, h]

    ri = lax.broadcasted_iota(I32, (c, c), 0)
    ci = lax.broadcasted_iota(I32, (c, c), 1)
    tri = (ci <= ri).astype(BF16)
    ones_cols = jnp.ones((c, V7X_LANES), BF16)
    k_shape = (N_HEADS * c, GLA_QK_W)
    k_head_mask = _iota_div(k_shape, 0, c) == _iota_div(k_shape, 1, GLA_DK)
    v_shape = (N_HEADS * c, GLA_WIDTH)
    v_head_mask = _iota_div(v_shape, 0, c) == _iota_div(v_shape, 1, GLA_DV)
    s_shape = (GLA_QK_W, GLA_WIDTH)
    s_head_mask = _iota_div(s_shape, 0, GLA_DK) == _iota_div(s_shape, 1, GLA_DV)
    a_shape = (c, N_HEADS * c)
    causal = _iota_mod(a_shape, 1, c) <= lax.broadcasted_iota(I32, a_shape, 0)
    row_id = lax.broadcasted_iota(I32, (c, GLA_QK_W), 0)
    mid = c // 2 - 1

    state = state_ref[...]
    for sub in range(n_sub):
        rows = slice(sub * c, (sub + 1) * c)
        q = q_ref[0, rows, :]
        k = k_ref[0, rows, :]
        g = g_ref[0, rows, :]
        v = v_ref[0, rows, :]
        if valid is not None:
            g = jnp.where(row_id < valid, g, 0.0)
        g_hi, g_lo = _split_bf16(g)
        b = _dot(tri, g_hi) + _dot(tri, g_lo)
        b_last_col = _dot_tn(g_hi, ones_cols) + _dot_tn(g_lo, ones_cols)
        b_last = b[c - 1:c, :]
        b_mid = b[mid:mid + 1, :]

        q_dec = (q * jnp.exp(b)).astype(BF16)
        o_inter = _dot(q_dec, state.astype(BF16))

        q_t = (q * jnp.exp(jnp.minimum(b - b_mid, EXP_CLAMP))).astype(BF16)
        k_t = (k * jnp.exp(jnp.minimum(b_mid - b, EXP_CLAMP))).astype(BF16)
        k_rows = jnp.where(k_head_mask, jnp.concatenate([k_t] * N_HEADS, axis=0), 0)
        a = _dot_nt(q_t, k_rows)
        a = jnp.where(causal, a, 0.0).astype(BF16)
        v_bf = v.astype(BF16)
        v_rows = jnp.where(v_head_mask, jnp.concatenate([v_bf] * N_HEADS, axis=0), 0)
        o_ref[0, rows, :] = o_inter + _dot(a, v_rows)

        k_dec = (k * jnp.exp(b_last - b)).astype(BF16)
        ds = _dot_tn(k_dec, v_bf)
        decay = jnp.exp(jnp.concatenate([b_last_col] * (GLA_WIDTH // V7X_LANES), axis=1))
        state = decay * state + jnp.where(s_head_mask, ds, 0.0)
    state_ref[...] = state

    @pl.when(t == pl.num_programs(1) - 1)
    def _():
        for h in range(N_HEADS):
            sfin_ref[0, h] = state_ref[GLA_DK * h:GLA_DK * (h + 1), GLA_DV * h:GLA_DV * (h + 1)]


def _gla(gq, gk, glog, gv, s0, *, chunk, n_sub, valid=None):
    bsz, tlen, _ = gq.shape
    tb = chunk * n_sub
    assert tlen % tb == 0
    s0_map = (lambda b, t: (b, 0, 0, 0)) if s0.shape[0] == bsz else (lambda b, t: (0, 0, 0, 0))
    seq = lambda w: pl.BlockSpec((1, tb, w), lambda b, t: (b, t, 0))
    st = (1, N_HEADS, GLA_DK, GLA_DV)
    return pl.pallas_call(
        functools.partial(_gla_body, chunk=chunk, n_sub=n_sub, valid=valid),
        grid=(bsz, tlen // tb),
        in_specs=[seq(GLA_QK_W), seq(GLA_QK_W), seq(GLA_QK_W), seq(GLA_WIDTH), pl.BlockSpec(st, s0_map)],
        out_specs=[seq(GLA_WIDTH), pl.BlockSpec(st, lambda b, t: (b, 0, 0, 0))],
        out_shape=[jax.ShapeDtypeStruct((bsz, tlen, GLA_WIDTH), F32),
                   jax.ShapeDtypeStruct((bsz, N_HEADS, GLA_DK, GLA_DV), F32)],
        scratch_shapes=[pltpu.VMEM((GLA_QK_W, GLA_WIDTH), F32)],
        compiler_params=_cparams(("arbitrary", "arbitrary")),
        name="gla_scan",
    )(gq, gk, glog, gv, s0)


def _attn_body(lam_ref, bound_ref, q_ref, km_ref, vm_ref, k_ref, v_ref, o_ref, l_ref, acc_ref, *, tq):
    i = pl.program_id(2)
    q = q_ref[0]
    lane = lax.broadcasted_iota(I32, (tq, DIFF_DV), 1)
    zero = jnp.zeros_like(q)
    qs = jnp.concatenate([jnp.where(lane < DIFF_DQK, q, zero), jnp.where(lane >= DIFF_DQK, q, zero)], axis=0)
    diag_mask = lax.broadcasted_iota(I32, (2 * tq, tq), 1) <= _iota_mod((2 * tq, tq), 0, tq)
    bound = bound_ref[0]
    n_meta = km_ref.shape[0]

    def keys(j):
        return k_ref[0, pl.ds(pl.multiple_of(j * tq, tq), tq), :]

    def values(j):
        return v_ref[0, pl.ds(pl.multiple_of(j * tq, tq), tq), :]

    def finish(acc, l):
        o = acc / l
        o_ref[0] = o[:tq] - lam_ref[0] * o[tq:]

    @pl.when(bound <= SAFE_SCORE_BOUND)
    def _():
        def fold(p):
            return sum(p[:, V7X_LANES * c:V7X_LANES * (c + 1)] for c in range(tq // V7X_LANES))

        l_ref[...] = jnp.zeros_like(l_ref)
        acc_ref[...] = jnp.zeros_like(acc_ref)

        def full_chunk(j, carry):
            p = jnp.exp(_dot_nt(qs, keys(j)) - bound)
            l_ref[...] += fold(p)
            acc_ref[...] += _dot(p.astype(BF16), values(j))
            return carry

        lax.fori_loop(0, i, full_chunk, 0)
        p = jnp.where(diag_mask, jnp.exp(_dot_nt(qs, keys(i)) - bound), 0.0)
        pm = jnp.exp(_dot_nt(qs, km_ref[...]) - bound)
        acc = acc_ref[...] + _dot(p.astype(BF16), values(i)) + _dot(pm.astype(BF16), vm_ref[...])
        l = jnp.sum(l_ref[...] + fold(p), axis=-1, keepdims=True) + jnp.sum(pm, axis=-1, keepdims=True)
        finish(acc, l)

    @pl.when(bound > SAFE_SCORE_BOUND)
    def _():
        def step(k, v, carry, mask):
            m, l, acc = carry
            s = _dot_nt(qs, k)
            if mask is not None:
                s = jnp.where(mask, s, NEG_BIG)
            m_new = jnp.maximum(m, jnp.max(s, axis=-1, keepdims=True))
            alpha = jnp.exp(m - m_new)
            p = jnp.exp(s - m_new)
            l = alpha * l + jnp.sum(p, axis=-1, keepdims=True)
            acc = alpha * acc + _dot(p.astype(BF16), v)
            return m_new, l, acc

        carry = (jnp.full((2 * tq, 1), NEG_BIG, F32), jnp.zeros((2 * tq, 1), F32),
                 jnp.zeros((2 * tq, DIFF_DV), F32))
        carry = step(km_ref[...], vm_ref[...], carry, None)
        carry = lax.fori_loop(0, i, lambda j, c: step(keys(j), values(j), c, None), carry)
        _, l, acc = step(keys(i), values(i), carry, diag_mask)
        finish(acc, l)


def _attn_prompt(lam, bound, dq, km, vm, kb, vb, *, tq):
    bsz, tlen, _ = dq.shape
    assert tlen % tq == 0 and tq % V7X_LANES == 0
    meta_spec = pl.BlockSpec((km.shape[0], DIFF_DV), lambda b, h, i: (0, h))
    kv_spec = pl.BlockSpec((1, tlen, DIFF_DV), lambda b, h, i: (b, 0, h))
    q_spec = pl.BlockSpec((1, tq, DIFF_DV), lambda b, h, i: (b, i, h))
    smem = pl.BlockSpec(memory_space=pltpu.SMEM)
    return pl.pallas_call(
        functools.partial(_attn_body, tq=tq),
        grid=(bsz, N_HEADS, tlen // tq),
        in_specs=[smem, smem, q_spec, meta_spec, meta_spec, kv_spec, kv_spec],
        out_specs=q_spec,
        out_shape=jax.ShapeDtypeStruct((bsz, tlen, DIFF_W), F32),
        scratch_shapes=[pltpu.VMEM((2 * tq, V7X_LANES), F32), pltpu.VMEM((2 * tq, DIFF_DV), F32)],
        compiler_params=_cparams(("arbitrary", "arbitrary", "arbitrary")),
        name="diff_attn_prompt",
    )(lam, bound, dq, km, vm, kb, vb)


def _attn_paged_body(pt_ref, lam_ref, q_ref, kn_ref, vn_ref, *rest, n_pages, qrows, tnew):
    del pt_ref
    k_refs = rest[:n_pages]
    v_refs = rest[n_pages:2 * n_pages]
    o_ref, m_ref, l_ref, acc_ref = rest[2 * n_pages:]
    g = pl.program_id(1)
    n_rows = 2 * N_HEADS * qrows
    tpad = kn_ref.shape[1]
    q = q_ref[0]
    lane = lax.broadcasted_iota(I32, (qrows, 2 * DIFF_DV), 1)
    blocks = []
    for h in range(N_HEADS):
        pair = q[0:qrows, 2 * DIFF_DV * (h // 2):2 * DIFF_DV * (h // 2 + 1)]
        for c in range(2):
            lo = DIFF_DV * (h % 2) + DIFF_DQK * c
            blocks.append(jnp.where((lane >= lo) & (lane < lo + DIFF_DQK), pair, 0.0))
    qr = jnp.concatenate(blocks, axis=0).astype(BF16)

    def pack_pairs(ref):
        half = ref.shape[0] // 2
        return jnp.concatenate([ref[pl.ds(0, half, stride=2), :], ref[pl.ds(1, half, stride=2), :]],
                               axis=1).astype(BF16)

    @pl.when(g == 0)
    def _():
        kn = jnp.concatenate([kn_ref[0, :, 0:2 * DIFF_DV], kn_ref[0, :, 2 * DIFF_DV:]], axis=0)
        vn = jnp.concatenate([vn_ref[0, :, 0:2 * DIFF_DV], vn_ref[0, :, 2 * DIFF_DV:]], axis=0)
        shape = (n_rows, 2 * tpad)
        same_pair = _iota_div(shape, 0, 4 * qrows) == _iota_div(shape, 1, tpad)
        tok = _iota_mod(shape, 1, tpad)
        visible = same_pair & (tok <= _iota_mod(shape, 0, qrows)) & (tok < tnew)
        s = jnp.where(visible, _dot_nt(qr, kn), NEG_BIG)
        m = jnp.max(s, axis=-1, keepdims=True)
        p = jnp.exp(s - m)
        m_ref[...] = m
        l_ref[...] = jnp.sum(p, axis=-1, keepdims=True)
        acc_ref[...] = _dot(p.astype(BF16), vn)

    tile = (n_rows, V7X_LANES)
    pair_bias = jnp.where(_iota_mod(tile, 1, 2) == _iota_div(tile, 0, 4 * qrows), 0.0, NEG_BIG)
    cols = k_refs[0].shape[0] // 2
    bias = jnp.concatenate([pair_bias] * (cols // V7X_LANES), axis=1)
    m, l, acc = m_ref[...], l_ref[...], acc_ref[...]
    for first in range(0, n_pages, PAGE_GROUP):
        group = range(first, min(first + PAGE_GROUP, n_pages))
        scores = [_dot_nt(qr, pack_pairs(k_refs[j])) + bias for j in group]
        m_new = m
        for s in scores:
            m_new = jnp.maximum(m_new, jnp.max(s, axis=-1, keepdims=True))
        alpha = jnp.exp(m - m_new)
        l, acc = alpha * l, alpha * acc
        for j, s in zip(group, scores):
            p = jnp.exp(s - m_new)
            l = l + jnp.sum(p, axis=-1, keepdims=True)
            acc = acc + _dot(p.astype(BF16), pack_pairs(v_refs[j]))
        m = m_new
    m_ref[...], l_ref[...], acc_ref[...] = m, l, acc

    @pl.when(g == pl.num_programs(1) - 1)
    def _():
        o = acc_ref[...] / l_ref[...]
        lam = lam_ref[0]
        pad = jnp.zeros((o_ref.shape[1] - qrows, DIFF_DV), F32)
        for h in range(N_HEADS):
            r0 = 2 * qrows * h
            half = slice(DIFF_DV * (h % 2), DIFF_DV * (h % 2 + 1))
            res = o[r0:r0 + qrows, half] - lam * o[r0 + qrows:r0 + 2 * qrows, half]
            o_ref[0, :, DIFF_DV * h:DIFF_DV * (h + 1)] = jnp.concatenate([res, pad], axis=0)


def _attn_paged(page_table, lam, dq, kn, vn, cache_k, cache_v, *, page, tnew):
    dbsz, tpad, _ = dq.shape
    n_tbl = page_table.shape[1]
    qrows = 1 << (tnew - 1).bit_length()
    orows = V7X_SUBLANES
    assert tnew <= qrows <= orows <= tpad
    n_pages = PAGES_PER_STEP
    while n_tbl % n_pages:
        n_pages //= 2
    n_rows = 2 * N_HEADS * qrows
    tok_spec = pl.BlockSpec((1, tpad, DIFF_W), lambda b, g, pt: (b, 0, 0))
    out_spec = pl.BlockSpec((1, orows, DIFF_W), lambda b, g, pt: (b, 0, 0))

    def page_spec(j):
        return pl.BlockSpec((page * N_HEADS, DIFF_DV), lambda b, g, pt: (pt[b, g * n_pages + j], 0))

    grid_spec = pltpu.PrefetchScalarGridSpec(
        num_scalar_prefetch=1,
        grid=(dbsz, n_tbl // n_pages),
        in_specs=[pl.BlockSpec(memory_space=pltpu.SMEM), tok_spec, tok_spec, tok_spec]
        + [page_spec(j) for j in range(n_pages)] * 2,
        out_specs=out_spec,
        scratch_shapes=[pltpu.VMEM((n_rows, 1), F32), pltpu.VMEM((n_rows, 1), F32),
                        pltpu.VMEM((n_rows, 2 * DIFF_DV), F32)],
    )
    return pl.pallas_call(
        functools.partial(_attn_paged_body, n_pages=n_pages, qrows=qrows, tnew=tnew),
        grid_spec=grid_spec,
        out_shape=jax.ShapeDtypeStruct((dbsz, orows, DIFF_W), F32),
        compiler_params=_cparams(("arbitrary", "arbitrary")),
        name="diff_attn_paged",
    )(page_table, lam, dq, kn, vn, *([cache_k] * n_pages), *([cache_v] * n_pages))


N_MIX_INPUTS = 11


def _mix_body(*refs, n_blocks):
    ins, outs = refs[:N_MIX_INPUTS], refs[-4:]
    i = pl.program_id(0)

    @pl.when(i < n_blocks)
    def _():
        _mix_tokens(*ins, *outs)

    @pl.when(i >= n_blocks)
    def _():
        for r in outs:
            r[...] = jnp.zeros_like(r)


def _mix_tokens(og_ref, gg_ref, od_ref, x_ref, ggain_ref, dgain_ref, wo_ref, n2_ref, wrh_ref, wrl_ref, br_ref,
                h_ref, xn_ref, ei_ref, gt_ref):
    def head_norm(z, gain):
        parts = []
        for h in range(N_HEADS):
            seg = z[:, GLA_DV * h:GLA_DV * (h + 1)]
            parts.append(seg * lax.rsqrt(jnp.mean(seg * seg, axis=-1, keepdims=True) + EPS))
        return jnp.concatenate(parts, axis=-1) * gain

    gg = gg_ref[...]
    a = head_norm(og_ref[...], ggain_ref[...]) * (gg / (1.0 + jnp.exp(-gg)))
    d = head_norm(od_ref[...], dgain_ref[...]) * (1.0 - LAM_INIT)
    cat = jnp.concatenate([a, d], axis=-1).astype(BF16)
    hres = x_ref[...] + _dot(cat, wo_ref[...])
    h_ref[...] = hres

    xn = hres * lax.rsqrt(jnp.mean(hres * hres, axis=-1, keepdims=True) + EPS) * n2_ref[...]
    _store_row_tiles(xn_ref, xn)

    x_hi, x_lo = _split_bf16(xn)
    logits = _dot(x_hi, wrh_ref[...]) + _dot(x_lo, wrh_ref[...]) + _dot(x_hi, wrl_ref[...]) + br_ref[...]
    lane = lax.broadcasted_iota(I32, logits.shape, 1).astype(F32)
    far = jnp.float32(1e4)

    def rmax(z):
        return jnp.max(z, axis=-1, keepdims=True)

    def first_lane(hit):
        return jnp.min(jnp.where(hit, lane, far), axis=-1, keepdims=True)

    is_group = lane < N_GROUPS
    lg = jnp.where(is_group, logits, NEG_BIG)
    mg = rmax(lg)
    grp = first_lane(lg == mg)
    g_gate = 1.0 / jnp.sum(jnp.where(is_group, jnp.exp(lg - mg), 0.0), axis=-1, keepdims=True)
    lo = N_GROUPS + EXPERTS_PER_GROUP * grp
    in_grp = (lane >= lo) & (lane < lo + EXPERTS_PER_GROUP)
    le = jnp.where(in_grp, logits, NEG_BIG)
    v1 = rmax(le)
    i1 = first_lane(in_grp & (le == v1))
    rest = in_grp & (lane != i1)
    le2 = jnp.where(rest, logits, NEG_BIG)
    v2 = rmax(le2)
    i2 = first_lane(rest & (le2 == v2))
    e21 = jnp.exp(v2 - v1)
    p1 = 1.0 / (1.0 + e21)
    ei = jnp.where(lane == 0, i1 - N_GROUPS, jnp.where(lane == 1, i2 - N_GROUPS, 0.0))
    ei_ref[...] = ei.astype(I32)
    gt_ref[...] = jnp.where(lane == 0, g_gate * p1, jnp.where(lane == 1, g_gate * (e21 * p1), 0.0))


def _mix_out(o_gla, gg, o_diff, x2d, prep, tile, n_total, row_start=0, base=None):
    n = x2d.shape[0]
    tm = _row_tile(n, tile)
    assert row_start % tm == 0 and n_total % tm == 0
    off = row_start // tm
    n_blocks = n // tm
    n_steps = n_blocks if base is not None else n_total // tm - off
    row = lambda w: pl.BlockSpec((tm, w), lambda i: (jnp.minimum(i, n_blocks - 1), 0))
    out_row = lambda w: pl.BlockSpec((tm, w), lambda i: (i + off, 0))
    full = lambda a: pl.BlockSpec(a.shape, lambda i: (0,) * a.ndim)
    consts = (prep["gla_gain"], prep["diff_gain"], prep["w_out"], prep["norm2"], prep["w_route_hi"],
              prep["w_route_lo"], prep["b_route"])
    base = () if base is None else tuple(base)
    n_in = 4 + len(consts)
    assert n_in == N_MIX_INPUTS
    return pl.pallas_call(
        functools.partial(_mix_body, n_blocks=n_blocks),
        grid=(n_steps,),
        in_specs=[row(GLA_WIDTH), row(GLA_WIDTH), row(DIFF_W), row(D_MODEL)] + [full(c) for c in consts]
        + [pl.BlockSpec(memory_space=pl.ANY)] * len(base),
        out_specs=[out_row(D_MODEL), pl.BlockSpec((tm * ROW_TILE, V7X_LANES), lambda i: (i + off, 0)),
                   out_row(V7X_LANES), out_row(V7X_LANES)],
        out_shape=[jax.ShapeDtypeStruct((n_total, D_MODEL), F32),
                   jax.ShapeDtypeStruct((n_total * ROW_TILE, V7X_LANES), F32),
                   jax.ShapeDtypeStruct((n_total, V7X_LANES), I32), jax.ShapeDtypeStruct((n_total, V7X_LANES), F32)],
        input_output_aliases={n_in + j: j for j in range(len(base))},
        compiler_params=_cparams(("arbitrary",)),
        name="mix_out_route",
    )(o_gla, gg, o_diff, x2d, *consts, *base)


def _rank_body(ei_ref, rank_ref, cnt_ref, carry_ref):
    i = pl.program_id(0)

    @pl.when(i == 0)
    def _():
        carry_ref[...] = jnp.zeros_like(carry_ref)

    ei = ei_ref[...]
    tb = ei.shape[0]
    lane = lax.broadcasted_iota(I32, ei.shape, 1)
    e0 = ei[:, 0:1]
    e1 = ei[:, 1:2]
    oh0 = lane == e0
    oh1 = lane == e1
    cnt = oh0.astype(F32) + oh1.astype(F32)
    ri = lax.broadcasted_iota(I32, (tb, tb), 0)
    ci = lax.broadcasted_iota(I32, (tb, tb), 1)
    before = _dot((ci < ri).astype(BF16), cnt.astype(BF16)) + carry_ref[0:1, :]
    r0 = jnp.sum(jnp.where(oh0, before, 0.0), axis=-1, keepdims=True)
    r1 = jnp.sum(jnp.where(oh1, before, 0.0), axis=-1, keepdims=True)
    rank_ref[...] = jnp.where(lane == 0, r0, jnp.where(lane == 1, r1, 0.0)).astype(I32)
    total = carry_ref[0:1, :] + jnp.sum(cnt, axis=0, keepdims=True)
    carry_ref[...] = jnp.broadcast_to(total, carry_ref.shape)
    cnt_ref[...] = jnp.broadcast_to(total, cnt_ref.shape).astype(I32)


def _rank(ei):
    n = ei.shape[0]
    tb = _row_tile(n, 256)
    row = pl.BlockSpec((tb, V7X_LANES), lambda i: (i, 0))
    one = pl.BlockSpec((V7X_SUBLANES, V7X_LANES), lambda i: (0, 0))
    return pl.pallas_call(
        _rank_body,
        grid=(n // tb,),
        in_specs=[row],
        out_specs=[row, one],
        out_shape=[jax.ShapeDtypeStruct((n, V7X_LANES), I32), jax.ShapeDtypeStruct((V7X_SUBLANES, V7X_LANES), I32)],
        scratch_shapes=[pltpu.VMEM((V7X_SUBLANES, V7X_LANES), F32)],
        compiler_params=_cparams(("arbitrary",)),
        name="moe_rank",
    )(ei)


def _tile_rows(ref, row):
    return ref.at[pl.ds(pl.multiple_of(row * ROW_TILE, ROW_TILE), ROW_TILE)]


def _issue_loop(n, per_item):
    assert n % DMA_ISSUE_UNROLL == 0

    def group(g, c):
        for u in range(DMA_ISSUE_UNROLL):
            per_item(g * DMA_ISSUE_UNROLL + u, u)
        return c

    lax.fori_loop(0, n // DMA_ISSUE_UNROLL, group, 0)


def _dispatch_body(dest_ref, zfill_ref, x_ref, xb_ref, zero_ref, sem, zsem, *, tb):
    i = pl.program_id(0)

    def zero_copy(e):
        start = pl.multiple_of(zfill_ref[e] * ROW_TILE, ROW_TILE)
        return pltpu.make_async_copy(zero_ref, xb_ref.at[pl.ds(start, MOE_BLOCK * ROW_TILE)], zsem)

    @pl.when(i == 0)
    def _():
        zero_ref[...] = jnp.zeros_like(zero_ref)
        for e in range(zfill_ref.shape[0]):
            @pl.when(zfill_ref[e] >= 0)
            def _():
                zero_copy(e).start()
        for e in range(zfill_ref.shape[0]):
            @pl.when(zfill_ref[e] >= 0)
            def _():
                zero_copy(e).wait()

    def copy(t, k):
        return pltpu.make_async_copy(_tile_rows(x_ref, t), _tile_rows(xb_ref, dest_ref[0, 0, 2 * t + k]), sem)

    def start(t, u):
        copy(t, 0).start(priority=0)
        copy(t, 1).start(priority=1)

    def wait(t, u):
        copy(t, 0).wait()
        copy(t, 1).wait()

    _issue_loop(tb, start)
    _issue_loop(tb, wait)


def _dispatch(dest, zfill, xn_tiles, cap):
    n = xn_tiles.shape[0] // ROW_TILE
    tb = _row_tile(n, 512)
    return pl.pallas_call(
        functools.partial(_dispatch_body, tb=tb),
        grid=(n // tb,),
        in_specs=[pl.BlockSpec((1, 1, 2 * tb), lambda i: (i, 0, 0), memory_space=pltpu.SMEM),
                  pl.BlockSpec(memory_space=pltpu.SMEM),
                  pl.BlockSpec((tb * ROW_TILE, V7X_LANES), lambda i: (i, 0))],
        out_specs=pl.BlockSpec(memory_space=pl.ANY),
        out_shape=jax.ShapeDtypeStruct((cap * ROW_TILE, V7X_LANES), F32),
        scratch_shapes=[pltpu.VMEM((MOE_BLOCK * ROW_TILE, V7X_LANES), F32), pltpu.SemaphoreType.DMA(()),
                        pltpu.SemaphoreType.DMA(())],
        compiler_params=_cparams(("arbitrary",)),
        name="moe_dispatch",
    )(dest.reshape(n // tb, 1, 2 * tb), zfill, xn_tiles)


def _expert_body(be_ref, bv_ref, xb_ref, w1_ref, w3_ref, w2_ref, yb_ref, w1b, w3b, w2b):
    i = pl.program_id(0)
    prev = be_ref[jnp.maximum(i - 1, 0)]

    @pl.when((i == 0) | (be_ref[i] != prev))
    def _():
        w1b[...] = w1_ref[0].astype(BF16)
        w3b[...] = w3_ref[0].astype(BF16)
        w2b[...] = w2_ref[0].astype(BF16)

    @pl.when(bv_ref[i] > 0)
    def _():
        x = _load_row_tiles(xb_ref, MOE_BLOCK).astype(BF16)
        up = _dot(x, w1b[...])
        hid = (up / (1.0 + jnp.exp(-up))) * _dot(x, w3b[...])
        _store_row_tiles(yb_ref, _dot(hid.astype(BF16), w2b[...]))

    @pl.when(bv_ref[i] == 0)
    def _():
        yb_ref[...] = jnp.zeros_like(yb_ref)


def _experts(blk_expert, blk_valid, xb, w_up, w_gate, w_down):
    cap = xb.shape[0] // ROW_TILE
    n_blk = cap // MOE_BLOCK
    w13 = pl.BlockSpec((1, D_MODEL, D_EXPERT), lambda i, be, bv: (be[i], 0, 0))
    w2 = pl.BlockSpec((1, D_EXPERT, D_MODEL), lambda i, be, bv: (be[i], 0, 0))
    rows = pl.BlockSpec((MOE_BLOCK * ROW_TILE, V7X_LANES), lambda i, be, bv: (i, 0))
    grid_spec = pltpu.PrefetchScalarGridSpec(
        num_scalar_prefetch=2,
        grid=(n_blk,),
        in_specs=[rows, w13, w13, w2],
        out_specs=rows,
        scratch_shapes=[pltpu.VMEM((D_MODEL, D_EXPERT), BF16), pltpu.VMEM((D_MODEL, D_EXPERT), BF16),
                        pltpu.VMEM((D_EXPERT, D_MODEL), BF16)],
    )
    return pl.pallas_call(
        _expert_body,
        grid_spec=grid_spec,
        out_shape=jax.ShapeDtypeStruct((cap * ROW_TILE, V7X_LANES), F32),
        compiler_params=_cparams(("arbitrary",)),
        name="moe_experts",
    )(blk_expert, blk_valid, xb, w_up, w_gate, w_down)


def _combine_body(dest_ref, next_ref, h_ref, gt_ref, yb_ref, o_ref, buf, sem, *, tb):
    i = pl.program_id(0)
    slot = i & 1

    def copy(idx_ref, s, t, k):
        return pltpu.make_async_copy(_tile_rows(yb_ref, idx_ref[0, 0, 2 * t + k]), _tile_rows(buf.at[s, k], t),
                                     sem.at[s])

    def issue(idx_ref, s):
        def start(t, u):
            copy(idx_ref, s, t, 0).start(priority=0)
            copy(idx_ref, s, t, 1).start(priority=1)

        _issue_loop(tb, start)

    @pl.when(i == 0)
    def _():
        issue(dest_ref, 0)

    @pl.when(i + 1 < pl.num_programs(0))
    def _():
        issue(next_ref, 1 - slot)

    def wait(t, u):
        copy(dest_ref, slot, t, 0).wait()
        copy(dest_ref, slot, t, 1).wait()

    _issue_loop(tb, wait)
    gt = gt_ref[...]
    y0 = _load_row_tiles(buf.at[slot, 0], tb)
    y1 = _load_row_tiles(buf.at[slot, 1], tb)
    o_ref[...] = h_ref[...] + gt[:, 0:1] * y0 + gt[:, 1:2] * y1


def _combine(dest, hres, gt, yb, row_start, n_rows):
    n_total = hres.shape[0]
    tb = _row_tile(n_rows, 256)
    assert row_start % tb == 0 and n_total % tb == 0
    off = row_start // tb
    n_steps = n_rows // tb
    idx = lambda f: pl.BlockSpec((1, 1, 2 * tb), lambda i: (f(i) + off, 0, 0), memory_space=pltpu.SMEM)
    return pl.pallas_call(
        functools.partial(_combine_body, tb=tb),
        grid=(n_steps,),
        in_specs=[idx(lambda i: i), idx(lambda i: jnp.minimum(i + 1, n_steps - 1)),
                  pl.BlockSpec((tb, D_MODEL), lambda i: (i + off, 0)),
                  pl.BlockSpec((tb, V7X_LANES), lambda i: (i + off, 0)),
                  pl.BlockSpec(memory_space=pl.ANY)],
        out_specs=pl.BlockSpec((tb, D_MODEL), lambda i: (i, 0)),
        out_shape=jax.ShapeDtypeStruct((n_rows, D_MODEL), F32),
        scratch_shapes=[pltpu.VMEM((2, 2, tb * ROW_TILE, V7X_LANES), F32), pltpu.SemaphoreType.DMA((2,))],
        compiler_params=_cparams(("arbitrary",)),
        name="moe_combine",
    )(dest.reshape(n_total // tb, 1, 2 * tb), dest.reshape(n_total // tb, 1, 2 * tb), hres, gt, yb)


def _moe(hres, xn, ei, gt, w_up, w_gate, w_down, splits):
    n = hres.shape[0]
    rank, counts = _rank(ei)
    counts = counts[0, :N_EXPERTS]
    padded = (counts + MOE_BLOCK - 1) // MOE_BLOCK * MOE_BLOCK
    pad_end = jnp.cumsum(padded)
    pad_start = pad_end - padded
    e2 = ei[:, :2]
    dest = (pad_start[e2] + rank[:, :2]).reshape(-1)
    n_blk = -(-(2 * n + N_EXPERTS * (MOE_BLOCK - 1)) // MOE_BLOCK)
    blk_start = jnp.arange(n_blk, dtype=I32) * MOE_BLOCK
    blk_expert = jnp.sum((pad_end[None, :] <= blk_start[:, None]).astype(I32), axis=1)
    blk_expert = jnp.minimum(blk_expert, N_EXPERTS - 1)
    blk_valid = (blk_start < (pad_start + counts)[blk_expert]).astype(I32)
    last_used = jnp.max(jnp.where(blk_valid > 0, blk_expert, 0))
    blk_expert = jnp.where(blk_start < pad_end[-1], blk_expert, last_used)
    tail_start = blk_start[n_blk - N_EXPERTS:]
    zfill = jnp.concatenate([jnp.where(padded > 0, pad_end - MOE_BLOCK, -1),
                             jnp.where(tail_start >= pad_end[-1], tail_start, -1)]).astype(I32)
    xb = _dispatch(dest, zfill, xn, n_blk * MOE_BLOCK)
    yb = _experts(blk_expert, blk_valid, xb, w_up, w_gate, w_down)
    return [_combine(dest, hres, gt, yb, start, rows) for start, rows in splits]


def _prepare(norm1, w_in, w_gk2, b_gk2, gla_norm, q_norm, k_norm, diff_norm, w_out, norm2,
             w_route_group, b_route_group, w_route_expert, b_route_expert):
    gq, gk, gv, gg, glr, dq, dk, dv = jnp.split(
        w_in, [256, 512, 1024, 1536, 1552, 2064, 2576], axis=-1)
    glr = jnp.pad(glr, ((0, 0), (0, V7X_LANES - GLA_RANK)))
    w_packed = jnp.concatenate([gq, gk, gv, gg, dq, dk, dv, glr], axis=-1).astype(BF16)
    w_route = jnp.concatenate([w_route_group, w_route_expert], axis=-1)
    w_route = jnp.pad(w_route, ((0, 0), (0, V7X_LANES - N_GROUPS - N_EXPERTS)))
    w_route_hi = w_route.astype(BF16)
    b_route = jnp.pad(jnp.concatenate([b_route_group, b_route_expert]), (0, V7X_LANES - N_GROUPS - N_EXPERTS))
    gidx = jnp.arange(256) // DIFF_DQK
    return {
        "norm1": norm1.reshape(1, D_MODEL),
        "w_in": w_packed,
        "w_gk2": jnp.pad(w_gk2, ((0, V7X_LANES - GLA_RANK), (0, 0))).astype(BF16),
        "b_gk2": b_gk2.reshape(1, GLA_QK_W),
        "q_gain": jnp.tile(q_norm.reshape(-1), N_HEADS).reshape(1, DIFF_W),
        "k_gain": jnp.tile(k_norm.reshape(-1), N_HEADS).reshape(1, DIFF_W),
        "group_ones": (gidx[:, None] == gidx[None, :]).astype(BF16),
        "gla_gain": jnp.tile(gla_norm, N_HEADS).reshape(1, GLA_WIDTH),
        "diff_gain": jnp.tile(diff_norm, N_HEADS).reshape(1, DIFF_W),
        "w_out": w_out.astype(BF16),
        "norm2": norm2.reshape(1, D_MODEL),
        "w_route_hi": w_route_hi,
        "w_route_lo": (w_route - w_route_hi.astype(F32)).astype(BF16),
        "b_route": b_route.reshape(1, V7X_LANES),
    }


def kernel(x_prompt, x_sample, cache_k, cache_v, state_gla, page_table, meta_tokens, norm1, w_in, w_gk2, b_gk2,
           gla_norm, q_norm, k_norm, lam_q1, lam_k1, lam_q2, lam_k2, diff_norm, w_out, norm2, w_route_group,
           b_route_group, w_route_expert, b_route_expert, w_up, w_gate, w_down):
    bsz, seq, _ = x_prompt.shape
    dbsz, dseq, _ = x_sample.shape
    n_phys, page = cache_k.shape[:2]
    prep = _prepare(norm1[0], w_in[0], w_gk2[0], b_gk2[0], gla_norm[0], q_norm[0], k_norm[0], diff_norm[0],
                    w_out[0], norm2[0], w_route_group[0], b_route_group[0], w_route_expert[0], b_route_expert[0])
    lam = (jnp.exp(jnp.sum(lam_q1[0] * lam_k1[0])) - jnp.exp(jnp.sum(lam_q2[0] * lam_k2[0])) + LAM_INIT).reshape(1)
    w_up, w_gate, w_down = w_up[0], w_gate[0], w_down[0]

    mq, mk, mv, _, mglog, _, mdkb, mdvb, mdk, mdv = _inproj(meta_tokens, prep, N_META)
    s_zero = jnp.zeros((1, N_HEADS, GLA_DK, GLA_DV), F32)
    _, s_meta = _gla(mq[None], mk[None], mglog[None], mv[None], s_zero, chunk=N_META, n_sub=1)

    xp = x_prompt.reshape(bsz * seq, D_MODEL)
    head_rows = lambda a: a.reshape(N_META * N_HEADS, DIFF_DV)
    gq, gk, gv, gg, glog, dq, dkb, dvb, k_rows, v_rows = _inproj(xp, prep, 512, seq=seq, meta_k=head_rows(mdk),
                                                                   meta_v=head_rows(mdv))
    seq3 = lambda a: a.reshape(bsz, seq, a.shape[-1])
    n_sub = GLA_SUBCHUNKS if seq % (GLA_SUBCHUNKS * GLA_CHUNK) == 0 else 1
    o_gla, gla_prompt = _gla(seq3(gq), seq3(gk), seq3(glog), seq3(gv), s_meta, chunk=GLA_CHUNK, n_sub=n_sub)
    k_prompt = k_rows.reshape(bsz, seq + N_META, 1, N_HEADS, DIFF_DV)
    v_prompt = v_rows.reshape(bsz, seq + N_META, 1, N_HEADS, DIFF_DV)
    bound = (8.1 * jnp.max(jnp.abs(q_norm[0])) * jnp.max(jnp.abs(k_norm[0]))).reshape(1)
    o_diff = _attn_prompt(lam, bound, seq3(dq), mdkb, mdvb, seq3(dkb), seq3(dvb), tq=_row_tile(seq, ATTN_TQ))
    n_p, n_s = bsz * seq, dbsz * dseq
    n_s_pad = -(-n_s // MOE_TOKEN_TILE) * MOE_TOKEN_TILE
    n_tok = n_p + n_s_pad
    tokens = _mix_out(o_gla.reshape(n_p, GLA_WIDTH), gg, o_diff.reshape(n_p, DIFF_W), xp, prep, MOE_TOKEN_TILE,
                      n_tok)

    tpad = -(-dseq // V7X_BF16_SUBLANES) * V7X_BF16_SUBLANES
    xs = jnp.pad(x_sample, ((0, 0), (0, tpad - dseq), (0, 0))).reshape(dbsz * tpad, D_MODEL)
    sq, sk, sv, sg, sglog, sdq, sdkb, sdvb, sdk, sdv = _inproj(xs, prep, 128)
    pad3 = lambda a: a.reshape(dbsz, tpad, a.shape[-1])
    unpad = lambda a: pad3(a)[:, :dseq]
    rows = lambda a: a.reshape(dbsz * dseq, a.shape[-1])
    o_gla_s, gla_sample = _gla(pad3(sq), pad3(sk), pad3(sglog), pad3(sv), state_gla[:, 0], chunk=tpad, n_sub=1,
                               valid=dseq)
    cache_rows = lambda c: c.reshape(n_phys * page * N_HEADS, DIFF_DV)
    o_diff_s = _attn_paged(page_table, lam, pad3(sdq).astype(F32), pad3(sdkb), pad3(sdvb), cache_rows(cache_k),
                           cache_rows(cache_v), page=page, tnew=dseq)
    tile_pad = lambda a: jnp.pad(rows(a), ((0, n_s_pad - n_s), (0, 0)))
    hres, xn_tiles, ei, gt = _mix_out(tile_pad(unpad(o_gla_s)), tile_pad(unpad(sg)), tile_pad(o_diff_s[:, :dseq]),
                                      tile_pad(x_sample), prep, MOE_TOKEN_TILE, n_tok, row_start=n_p, base=tokens)
    y_prompt, y_sample = _moe(hres, xn_tiles, ei, gt, w_up, w_gate, w_down, [(0, n_p), (n_p, n_s_pad)])
    y_prompt = y_prompt.reshape(bsz, seq, D_MODEL)
    y_sample = y_sample[:n_s].reshape(dbsz, dseq, D_MODEL)

    heads = lambda a: a.reshape(a.shape[0], a.shape[1], 1, N_HEADS, DIFF_DV)
    return (y_prompt, y_sample, k_prompt, v_prompt, gla_prompt[:, None],
            heads(unpad(sdk)), heads(unpad(sdv)), gla_sample[:, None])
```

```python
import functools

import jax
import jax.numpy as jnp
from jax import lax
from jax.experimental import pallas as pl
from jax.experimental.pallas import tpu as pltpu

F32 = jnp.float32
BF16 = jnp.bfloat16
I32 = jnp.int32

V7X_LANES = 128
V7X_SUBLANES = 8
V7X_BF16_SUBLANES = 16
V7X_VMEM_LIMIT_BYTES = 56 * 1024 * 1024

D_MODEL = 1024
N_META = 16
N_HEADS = 4
GLA_DK = 64
GLA_DV = 128
GLA_RANK = 16
GLA_TAU = 16.0
GLA_QK_W = N_HEADS * GLA_DK
GLA_WIDTH = N_HEADS * GLA_DV
DIFF_DQK = 64
DIFF_DV = 128
DIFF_W = N_HEADS * DIFF_DV
N_GROUPS = 4
EXPERTS_PER_GROUP = 8
N_EXPERTS = N_GROUPS * EXPERTS_PER_GROUP
D_EXPERT = D_MODEL // 2
EPS = 1e-6
LAM_INIT = 0.2
NEG_BIG = -1e30
EXP_CLAMP = 80.0

OFF_GQ, OFF_GK, OFF_GV, OFF_GG = 0, 256, 512, 1024
OFF_DQ, OFF_DK, OFF_DV, OFF_GLR = 1536, 2048, 2560, 3072
PACKED_IN_W = OFF_GLR + V7X_LANES

GLA_CHUNK = 64
GLA_SUBCHUNKS = 8
ATTN_TQ = 512
MOE_BLOCK = 256
MOE_TOKEN_TILE = 512
PAGES_PER_STEP = 32
PAGE_GROUP = 16
DMA_ISSUE_UNROLL = 8
SAFE_SCORE_BOUND = 40.0


def _cparams(sem):
    return pltpu.CompilerParams(dimension_semantics=sem, vmem_limit_bytes=V7X_VMEM_LIMIT_BYTES)


def _row_tile(n, pref):
    t = min(n, pref)
    while n % t:
        t //= 2
    return t


def _dot(a, b):
    return jnp.dot(a, b, preferred_element_type=F32)


def _dot_nt(a, b):
    return lax.dot_general(a, b, (((1,), (1,)), ((), ())), preferred_element_type=F32)


def _dot_tn(a, b):
    return lax.dot_general(a, b, (((0,), (0,)), ((), ())), preferred_element_type=F32)


def _iota_div(shape, dim, d):
    assert d & (d - 1) == 0
    return lax.shift_right_logical(lax.broadcasted_iota(I32, shape, dim), d.bit_length() - 1)


def _iota_mod(shape, dim, d):
    assert d & (d - 1) == 0
    return lax.broadcasted_iota(I32, shape, dim) & (d - 1)


ROW_TILE = D_MODEL // V7X_LANES


def _store_row_tiles(ref, x):
    rows = x.shape[0]
    for s in range(ROW_TILE):
        ref[pl.ds(s, rows, stride=ROW_TILE), :] = x[:, V7X_LANES * s:V7X_LANES * (s + 1)]


def _load_row_tiles(ref, rows):
    return jnp.concatenate([ref[pl.ds(s, rows, stride=ROW_TILE), :] for s in range(ROW_TILE)], axis=1)


def _split_bf16(x):
    hi = x.astype(BF16)
    lo = (x - hi.astype(F32)).astype(BF16)
    return hi, lo


def _inproj_body(*refs, seq_layout):
    x_ref, n1_ref, w_ref, wg2_ref, bg2_ref, qg_ref, kg_ref, gm_ref = refs[:8]
    if seq_layout is None:
        gq_ref, gk_ref, gv_ref, gg_ref, glog_ref, dq_ref, dkb_ref, dvb_ref, dk_ref, dv_ref = refs[8:]
    else:
        mk_ref, mv_ref = refs[8:10]
        gq_ref, gk_ref, gv_ref, gg_ref, glog_ref, dq_ref, dkb_ref, dvb_ref, kp_ref, vp_ref = refs[10:20]
        kbuf, vbuf, sem, msem = refs[20:]
    x = x_ref[...]
    ms = jnp.mean(x * x, axis=-1, keepdims=True)
    xn = (x * lax.rsqrt(ms + EPS) * n1_ref[...]).astype(BF16)

    def proj(off, width):
        return _dot(xn, w_ref[:, off:off + width])

    gq_ref[...] = proj(OFF_GQ, GLA_QK_W) * (GLA_DK ** -0.5)
    gk_ref[...] = proj(OFF_GK, GLA_QK_W)
    gv_ref[...] = proj(OFF_GV, GLA_WIDTH)
    gg_ref[...] = proj(OFF_GG, GLA_WIDTH)

    glr = proj(OFF_GLR, V7X_LANES).astype(BF16)
    gpre = _dot(glr, wg2_ref[...]) + bg2_ref[...]
    log_sig = jnp.minimum(gpre, 0.0) - jnp.log(1.0 + jnp.exp(-jnp.abs(gpre)))
    glog_ref[...] = log_sig * (1.0 / GLA_TAU)

    gm = gm_ref[...]

    def group_norm(z, gain):
        sq = (z * z).astype(BF16)
        ss = jnp.concatenate([_dot(sq[:, 256 * c:256 * (c + 1)], gm) for c in range(DIFF_W // 256)], axis=-1)
        return z * lax.rsqrt(ss * (1.0 / DIFF_DQK) + EPS) * gain

    dq = group_norm(proj(OFF_DQ, DIFF_W), qg_ref[...])
    dq_ref[...] = (dq * (DIFF_DQK ** -0.5)).astype(BF16)
    dk = group_norm(proj(OFF_DK, DIFF_W), kg_ref[...])
    dkb_ref[...] = dk.astype(BF16)
    dv = proj(OFF_DV, DIFF_W)
    dvb_ref[...] = dv.astype(BF16)
    if seq_layout is None:
        dk_ref[...] = dk
        dv_ref[...] = dv
        return

    seq, n_meta = seq_layout
    tm = x.shape[0]
    tiles_per_seq = seq // tm
    g = pl.program_id(0)
    slot = g & 1
    b = lax.shift_right_logical(g, tiles_per_seq.bit_length() - 1)
    seq_row0 = b * ((seq + n_meta) * N_HEADS)
    row0 = pl.multiple_of(seq_row0 + (n_meta + (g & (tiles_per_seq - 1)) * tm) * N_HEADS, V7X_SUBLANES)

    def copies(s):
        dst = pl.ds(row0, tm * N_HEADS)
        return (pltpu.make_async_copy(kbuf.at[s], kp_ref.at[dst], sem.at[s]),
                pltpu.make_async_copy(vbuf.at[s], vp_ref.at[dst], sem.at[s]))

    def meta_copies():
        dst = pl.ds(pl.multiple_of(seq_row0, V7X_SUBLANES), n_meta * N_HEADS)
        return (pltpu.make_async_copy(mk_ref, kp_ref.at[dst], msem),
                pltpu.make_async_copy(mv_ref, vp_ref.at[dst], msem))

    def wait_slot(s):
        for c in copies(s):
            c.wait()

    @pl.when(g >= 2)
    def _():
        wait_slot(slot)

    @pl.when((g & (tiles_per_seq - 1)) == 0)
    def _():
        for c in meta_copies():
            c.start()

    for h in range(N_HEADS):
        cols = slice(DIFF_DV * h, DIFF_DV * (h + 1))
        kbuf.at[slot][pl.ds(h, tm, stride=N_HEADS), :] = dk[:, cols]
        vbuf.at[slot][pl.ds(h, tm, stride=N_HEADS), :] = dv[:, cols]

    @pl.when((g & (tiles_per_seq - 1)) == 0)
    def _():
        for c in meta_copies():
            c.wait()

    for c in copies(slot):
        c.start()

    last = pl.num_programs(0) - 1

    @pl.when((g == last) & (g >= 1))
    def _():
        wait_slot(1 - slot)

    @pl.when(g == last)
    def _():
        wait_slot(slot)


def _inproj(x2d, prep, tile, seq=None, meta_k=None, meta_v=None):
    n = x2d.shape[0]
    tm = _row_tile(n, tile)
    row = lambda w: pl.BlockSpec((tm, w), lambda i: (i, 0))
    full = lambda a: pl.BlockSpec(a.shape, lambda i: (0,) * a.ndim)
    consts = (prep["norm1"], prep["w_in"], prep["w_gk2"], prep["b_gk2"], prep["q_gain"], prep["k_gain"],
              prep["group_ones"])
    widths = (GLA_QK_W, GLA_QK_W, GLA_WIDTH, GLA_WIDTH, GLA_QK_W, DIFF_W, DIFF_W, DIFF_W)
    dtypes = (F32, F32, F32, F32, F32, BF16, BF16, BF16)
    out_specs = [row(w) for w in widths]
    out_shape = [jax.ShapeDtypeStruct((n, w), d) for w, d in zip(widths, dtypes)]
    if seq is None:
        seq_layout, extra, scratch = None, (), []
        out_specs += [row(DIFF_W)] * 2
        out_shape += [jax.ShapeDtypeStruct((n, DIFF_W), F32)] * 2
    else:
        n_meta = meta_k.shape[0] // N_HEADS
        tiles_per_seq = seq // tm
        assert seq % tm == 0 and n % seq == 0 and tiles_per_seq & (tiles_per_seq - 1) == 0
        seq_layout, extra = (seq, n_meta), (meta_k, meta_v)
        final_rows = (n // seq) * (seq + n_meta) * N_HEADS
        out_specs += [pl.BlockSpec(memory_space=pl.ANY)] * 2
        out_shape += [jax.ShapeDtypeStruct((final_rows, DIFF_DV), F32)] * 2
        scratch = [pltpu.VMEM((2, tm * N_HEADS, DIFF_DV), F32), pltpu.VMEM((2, tm * N_HEADS, DIFF_DV), F32),
                   pltpu.SemaphoreType.DMA((2,)), pltpu.SemaphoreType.DMA(())]
    return pl.pallas_call(
        functools.partial(_inproj_body, seq_layout=seq_layout),
        grid=(n // tm,),
        in_specs=[row(D_MODEL)] + [full(c) for c in consts + extra],
        out_specs=out_specs,
        out_shape=out_shape,
        scratch_shapes=scratch,
        compiler_params=_cparams(("arbitrary",)),
        name="inproj",
    )(x2d, *consts, *extra)


def _gla_body(q_ref, k_ref, g_ref, v_ref, s0_ref, o_ref, sfin_ref, state_ref, *, chunk, n_sub, valid):
    t = pl.program_id(1)
    _gla_load_state(t == 0, s0_ref, state_ref)
    _gla_chunks(q_ref, k_ref, g_ref, v_ref, o_ref, state_ref, chunk=chunk, n_sub=n_sub, valid=valid)
    _gla_store_state(t == pl.num_programs(1) - 1, sfin_ref, state_ref)


def _gla_load_state(first, s0_ref, state_ref):
    @pl.when(first)
    def _():
        state_ref[...] = jnp.zeros_like(state_ref)
        for h in range(N_HEADS):
            state_ref[GLA_DK * h:GLA_DK * (h + 1), GLA_DV * h:GLA_DV * (h + 1)] = s0_ref[0, h]


def _gla_store_state(last, sfin_ref, state_ref):
    @pl.when(last)
    def _():
        for h in range(N_HEADS):
            sfin_ref[0, h] = state_ref[GLA_DK * h:GLA_DK * (h + 1), GLA_DV * h:GLA_DV * (h + 1)]


def _gla_chunks(q_ref, k_ref, g_ref, v_ref, o_ref, state_ref, *, chunk, n_sub, valid):
    c = chunk
    ri = lax.broadcasted_iota(I32, (c, c), 0)
    ci = lax.broadcasted_iota(I32, (c, c), 1)
    tri = (ci <= ri).astype(BF16)
    ones_cols = jnp.ones((c, V7X_LANES), BF16)
    k_shape = (N_HEADS * c, GLA_QK_W)
    k_head_mask = _iota_div(k_shape, 0, c) == _iota_div(k_shape, 1, GLA_DK)
    v_shape = (N_HEADS * c, GLA_WIDTH)
    v_head_mask = _iota_div(v_shape, 0, c) == _iota_div(v_shape, 1, GLA_DV)
    s_shape = (GLA_QK_W, GLA_WIDTH)
    s_head_mask = _iota_div(s_shape, 0, GLA_DK) == _iota_div(s_shape, 1, GLA_DV)
    a_shape = (c, N_HEADS * c)
    causal = _iota_mod(a_shape, 1, c) <= lax.broadcasted_iota(I32, a_shape, 0)
    row_id = lax.broadcasted_iota(I32, (c, GLA_QK_W), 0)
    mid = c // 2 - 1

    state = state_ref[...]
    for sub in range(n_sub):
        rows = slice(sub * c, (sub + 1) * c)
        q = q_ref[0, rows, :]
        k = k_ref[0, rows, :]
        g = g_ref[0, rows, :]
        v = v_ref[0, rows, :]
        if valid is not None:
            g = jnp.where(row_id < valid, g, 0.0)
        g_hi, g_lo = _split_bf16(g)
        b = _dot(tri, g_hi) + _dot(tri, g_lo)
        b_last_col = _dot_tn(g_hi, ones_cols) + _dot_tn(g_lo, ones_cols)
        b_last = b[c - 1:c, :]
        b_mid = b[mid:mid + 1, :]

        q_dec = (q * jnp.exp(b)).astype(BF16)
        o_inter = _dot(q_dec, state.astype(BF16))

        q_t = (q * jnp.exp(jnp.minimum(b - b_mid, EXP_CLAMP))).astype(BF16)
        k_t = (k * jnp.exp(jnp.minimum(b_mid - b, EXP_CLAMP))).astype(BF16)
        k_rows = jnp.where(k_head_mask, jnp.concatenate([k_t] * N_HEADS, axis=0), 0)
        a = _dot_nt(q_t, k_rows)
        a = jnp.where(causal, a, 0.0).astype(BF16)
        v_bf = v.astype(BF16)
        v_rows = jnp.where(v_head_mask, jnp.concatenate([v_bf] * N_HEADS, axis=0), 0)
        o_ref[0, rows, :] = o_inter + _dot(a, v_rows)

        k_dec = (k * jnp.exp(b_last - b)).astype(BF16)
        ds = _dot_tn(k_dec, v_bf)
        decay = jnp.exp(jnp.concatenate([b_last_col] * (GLA_WIDTH // V7X_LANES), axis=1))
        state = decay * state + jnp.where(s_head_mask, ds, 0.0)
    state_ref[...] = state


def _gla(gq, gk, glog, gv, s0, *, chunk, n_sub, valid=None):
    bsz, tlen, _ = gq.shape
    tb = chunk * n_sub
    assert tlen % tb == 0
    s0_map = (lambda b, t: (b, 0, 0, 0)) if s0.shape[0] == bsz else (lambda b, t: (0, 0, 0, 0))
    seq = lambda w: pl.BlockSpec((1, tb, w), lambda b, t: (b, t, 0))
    st = (1, N_HEADS, GLA_DK, GLA_DV)
    return pl.pallas_call(
        functools.partial(_gla_body, chunk=chunk, n_sub=n_sub, valid=valid),
        grid=(bsz, tlen // tb),
        in_specs=[seq(GLA_QK_W), seq(GLA_QK_W), seq(GLA_QK_W), seq(GLA_WIDTH), pl.BlockSpec(st, s0_map)],
        out_specs=[seq(GLA_WIDTH), pl.BlockSpec(st, lambda b, t: (b, 0, 0, 0))],
        out_shape=[jax.ShapeDtypeStruct((bsz, tlen, GLA_WIDTH), F32),
                   jax.ShapeDtypeStruct((bsz, N_HEADS, GLA_DK, GLA_DV), F32)],
        scratch_shapes=[pltpu.VMEM((GLA_QK_W, GLA_WIDTH), F32)],
        compiler_params=_cparams(("arbitrary", "arbitrary")),
        name="gla_scan",
    )(gq, gk, glog, gv, s0)


def _attn_body(lam_ref, bound_ref, q_ref, km_ref, vm_ref, k_ref, v_ref, o_ref, l_ref, acc_ref, *, tq):
    i = pl.program_id(2)
    q = q_ref[0]
    lane = lax.broadcasted_iota(I32, (tq, DIFF_DV), 1)
    zero = jnp.zeros_like(q)
    qs = jnp.concatenate([jnp.where(lane < DIFF_DQK, q, zero), jnp.where(lane >= DIFF_DQK, q, zero)], axis=0)
    diag_mask = lax.broadcasted_iota(I32, (2 * tq, tq), 1) <= _iota_mod((2 * tq, tq), 0, tq)
    bound = bound_ref[0]
    n_meta = km_ref.shape[0]

    def keys(j):
        return k_ref[0, pl.ds(pl.multiple_of(j * tq, tq), tq), :]

    def values(j):
        return v_ref[0, pl.ds(pl.multiple_of(j * tq, tq), tq), :]

    def finish(acc, l):
        o = acc / l
        o_ref[0] = o[:tq] - lam_ref[0] * o[tq:]

    @pl.when(bound <= SAFE_SCORE_BOUND)
    def _():
        def fold(p):
            return sum(p[:, V7X_LANES * c:V7X_LANES * (c + 1)] for c in range(tq // V7X_LANES))

        l_ref[...] = jnp.zeros_like(l_ref)
        acc_ref[...] = jnp.zeros_like(acc_ref)

        def full_chunk(j, carry):
            p = jnp.exp(_dot_nt(qs, keys(j)) - bound)
            l_ref[...] += fold(p)
            acc_ref[...] += _dot(p.astype(BF16), values(j))
            return carry

        lax.fori_loop(0, i, full_chunk, 0)
        p = jnp.where(diag_mask, jnp.exp(_dot_nt(qs, keys(i)) - bound), 0.0)
        pm = jnp.exp(_dot_nt(qs, km_ref[...]) - bound)
        acc = acc_ref[...] + _dot(p.astype(BF16), values(i)) + _dot(pm.astype(BF16), vm_ref[...])
        l = jnp.sum(l_ref[...] + fold(p), axis=-1, keepdims=True) + jnp.sum(pm, axis=-1, keepdims=True)
        finish(acc, l)

    @pl.when(bound > SAFE_SCORE_BOUND)
    def _():
        def step(k, v, carry, mask):
            m, l, acc = carry
            s = _dot_nt(qs, k)
            if mask is not None:
                s = jnp.where(mask, s, NEG_BIG)
            m_new = jnp.maximum(m, jnp.max(s, axis=-1, keepdims=True))
            alpha = jnp.exp(m - m_new)
            p = jnp.exp(s - m_new)
            l = alpha * l + jnp.sum(p, axis=-1, keepdims=True)
            acc = alpha * acc + _dot(p.astype(BF16), v)
            return m_new, l, acc

        carry = (jnp.full((2 * tq, 1), NEG_BIG, F32), jnp.zeros((2 * tq, 1), F32),
                 jnp.zeros((2 * tq, DIFF_DV), F32))
        carry = step(km_ref[...], vm_ref[...], carry, None)
        carry = lax.fori_loop(0, i, lambda j, c: step(keys(j), values(j), c, None), carry)
        _, l, acc = step(keys(i), values(i), carry, diag_mask)
        finish(acc, l)


def _attn_prompt(lam, bound, dq, km, vm, kb, vb, *, tq):
    bsz, tlen, _ = dq.shape
    assert tlen % tq == 0 and tq % V7X_LANES == 0
    meta_spec = pl.BlockSpec((km.shape[0], DIFF_DV), lambda b, h, i: (0, h))
    kv_spec = pl.BlockSpec((1, tlen, DIFF_DV), lambda b, h, i: (b, 0, h))
    q_spec = pl.BlockSpec((1, tq, DIFF_DV), lambda b, h, i: (b, i, h))
    smem = pl.BlockSpec(memory_space=pltpu.SMEM)
    return pl.pallas_call(
        functools.partial(_attn_body, tq=tq),
        grid=(bsz, N_HEADS, tlen // tq),
        in_specs=[smem, smem, q_spec, meta_spec, meta_spec, kv_spec, kv_spec],
        out_specs=q_spec,
        out_shape=jax.ShapeDtypeStruct((bsz, tlen, DIFF_W), F32),
        scratch_shapes=[pltpu.VMEM((2 * tq, V7X_LANES), F32), pltpu.VMEM((2 * tq, DIFF_DV), F32)],
        compiler_params=_cparams(("arbitrary", "arbitrary", "arbitrary")),
        name="diff_attn_prompt",
    )(lam, bound, dq, km, vm, kb, vb)


def _attn_paged_body(pt_ref, lam_ref, q_ref, kn_ref, vn_ref, *rest, n_pages, qrows, tnew):
    del pt_ref
    k_refs = rest[:n_pages]
    v_refs = rest[n_pages:2 * n_pages]
    o_ref, m_ref, l_ref, acc_ref = rest[2 * n_pages:]
    g = pl.program_id(1)
    stats = (m_ref, l_ref, acc_ref)
    _paged_new_tokens(g == 0, q_ref, kn_ref, vn_ref, stats, qrows=qrows, tnew=tnew)
    _paged_pages(q_ref, k_refs, v_refs, stats, qrows=qrows)
    _paged_finish(g == pl.num_programs(1) - 1, lam_ref, o_ref, stats, qrows=qrows)


def _paged_queries(q_ref, qrows):
    q = q_ref[0]
    lane = lax.broadcasted_iota(I32, (qrows, 2 * DIFF_DV), 1)
    blocks = []
    for h in range(N_HEADS):
        pair = q[0:qrows, 2 * DIFF_DV * (h // 2):2 * DIFF_DV * (h // 2 + 1)]
        for c in range(2):
            lo = DIFF_DV * (h % 2) + DIFF_DQK * c
            blocks.append(jnp.where((lane >= lo) & (lane < lo + DIFF_DQK), pair, 0.0))
    return jnp.concatenate(blocks, axis=0).astype(BF16)


def _paged_new_tokens(first, q_ref, kn_ref, vn_ref, stats, *, qrows, tnew):
    m_ref, l_ref, acc_ref = stats
    n_rows = 2 * N_HEADS * qrows
    tpad = kn_ref.shape[1]

    @pl.when(first)
    def _():
        qr = _paged_queries(q_ref, qrows)
        kn = jnp.concatenate([kn_ref[0, :, 0:2 * DIFF_DV], kn_ref[0, :, 2 * DIFF_DV:]], axis=0)
        vn = jnp.concatenate([vn_ref[0, :, 0:2 * DIFF_DV], vn_ref[0, :, 2 * DIFF_DV:]], axis=0)
        shape = (n_rows, 2 * tpad)
        same_pair = _iota_div(shape, 0, 4 * qrows) == _iota_div(shape, 1, tpad)
        tok = _iota_mod(shape, 1, tpad)
        visible = same_pair & (tok <= _iota_mod(shape, 0, qrows)) & (tok < tnew)
        s = jnp.where(visible, _dot_nt(qr, kn), NEG_BIG)
        m = jnp.max(s, axis=-1, keepdims=True)
        p = jnp.exp(s - m)
        m_ref[...] = m
        l_ref[...] = jnp.sum(p, axis=-1, keepdims=True)
        acc_ref[...] = _dot(p.astype(BF16), vn)


def _paged_pages(q_ref, k_refs, v_refs, stats, *, qrows):
    m_ref, l_ref, acc_ref = stats
    n_rows = 2 * N_HEADS * qrows
    n_pages = len(k_refs)
    qr = _paged_queries(q_ref, qrows)

    def pack_pairs(ref):
        half = ref.shape[0] // 2
        return jnp.concatenate([ref[pl.ds(0, half, stride=2), :], ref[pl.ds(1, half, stride=2), :]],
                               axis=1).astype(BF16)

    tile = (n_rows, V7X_LANES)
    pair_bias = jnp.where(_iota_mod(tile, 1, 2) == _iota_div(tile, 0, 4 * qrows), 0.0, NEG_BIG)
    cols = k_refs[0].shape[0] // 2
    bias = jnp.concatenate([pair_bias] * (cols // V7X_LANES), axis=1)
    m, l, acc = m_ref[...], l_ref[...], acc_ref[...]
    for first in range(0, n_pages, PAGE_GROUP):
        group = range(first, min(first + PAGE_GROUP, n_pages))
        scores = [_dot_nt(qr, pack_pairs(k_refs[j])) + bias for j in group]
        m_new = m
        for s in scores:
            m_new = jnp.maximum(m_new, jnp.max(s, axis=-1, keepdims=True))
        alpha = jnp.exp(m - m_new)
        l, acc = alpha * l, alpha * acc
        for j, s in zip(group, scores):
            p = jnp.exp(s - m_new)
            l = l + jnp.sum(p, axis=-1, keepdims=True)
            acc = acc + _dot(p.astype(BF16), pack_pairs(v_refs[j]))
        m = m_new
    m_ref[...], l_ref[...], acc_ref[...] = m, l, acc


def _paged_finish(last, lam_ref, o_ref, stats, *, qrows):
    _, l_ref, acc_ref = stats

    @pl.when(last)
    def _():
        o = acc_ref[...] / l_ref[...]
        lam = lam_ref[0]
        pad = jnp.zeros((o_ref.shape[1] - qrows, DIFF_DV), F32)
        for h in range(N_HEADS):
            r0 = 2 * qrows * h
            half = slice(DIFF_DV * (h % 2), DIFF_DV * (h % 2 + 1))
            res = o[r0:r0 + qrows, half] - lam * o[r0 + qrows:r0 + 2 * qrows, half]
            o_ref[0, :, DIFF_DV * h:DIFF_DV * (h + 1)] = jnp.concatenate([res, pad], axis=0)


def _attn_paged(page_table, lam, dq, kn, vn, cache_k, cache_v, *, page, tnew):
    dbsz, tpad, _ = dq.shape
    n_tbl = page_table.shape[1]
    qrows = 1 << (tnew - 1).bit_length()
    orows = V7X_SUBLANES
    assert tnew <= qrows <= orows <= tpad
    n_pages = PAGES_PER_STEP
    while n_tbl % n_pages:
        n_pages //= 2
    n_rows = 2 * N_HEADS * qrows
    tok_spec = pl.BlockSpec((1, tpad, DIFF_W), lambda b, g, pt: (b, 0, 0))
    out_spec = pl.BlockSpec((1, orows, DIFF_W), lambda b, g, pt: (b, 0, 0))

    def page_spec(j):
        return pl.BlockSpec((page * N_HEADS, DIFF_DV), lambda b, g, pt: (pt[b, g * n_pages + j], 0))

    grid_spec = pltpu.PrefetchScalarGridSpec(
        num_scalar_prefetch=1,
        grid=(dbsz, n_tbl // n_pages),
        in_specs=[pl.BlockSpec(memory_space=pltpu.SMEM), tok_spec, tok_spec, tok_spec]
        + [page_spec(j) for j in range(n_pages)] * 2,
        out_specs=out_spec,
        scratch_shapes=[pltpu.VMEM((n_rows, 1), F32), pltpu.VMEM((n_rows, 1), F32),
                        pltpu.VMEM((n_rows, 2 * DIFF_DV), F32)],
    )
    return pl.pallas_call(
        functools.partial(_attn_paged_body, n_pages=n_pages, qrows=qrows, tnew=tnew),
        grid_spec=grid_spec,
        out_shape=jax.ShapeDtypeStruct((dbsz, orows, DIFF_W), F32),
        compiler_params=_cparams(("arbitrary", "arbitrary")),
        name="diff_attn_paged",
    )(page_table, lam, dq, kn, vn, *([cache_k] * n_pages), *([cache_v] * n_pages))


def _gla_paged_body(pt_ref, lam_ref, gq_ref, gk_ref, gg_ref, gv_ref, s0_ref, q_ref, kn_ref, vn_ref, *rest,
                    n_pages, qrows, tnew, chunk, n_sub, gla_steps, paged_steps):
    del pt_ref
    k_refs = rest[:n_pages]
    v_refs = rest[n_pages:2 * n_pages]
    og_ref, sfin_ref, oa_ref, state_ref, m_ref, l_ref, acc_ref = rest[2 * n_pages:]
    step = pl.program_id(0)
    t = lax.rem(step, gla_steps)
    g = lax.rem(step, paged_steps)
    stats = (m_ref, l_ref, acc_ref)
    _gla_load_state(t == 0, s0_ref, state_ref)
    _paged_new_tokens(g == 0, q_ref, kn_ref, vn_ref, stats, qrows=qrows, tnew=tnew)
    _gla_chunks(gq_ref, gk_ref, gg_ref, gv_ref, og_ref, state_ref, chunk=chunk, n_sub=n_sub, valid=None)
    _paged_pages(q_ref, k_refs, v_refs, stats, qrows=qrows)
    _gla_store_state(t == gla_steps - 1, sfin_ref, state_ref)
    _paged_finish(g == paged_steps - 1, lam_ref, oa_ref, stats, qrows=qrows)


def _gla_and_paged(gq, gk, glog, gv, s0, page_table, lam, dq, kn, vn, cache_k, cache_v, *, chunk, n_sub, n_pages,
                   page, tnew):
    bsz, tlen, _ = gq.shape
    dbsz, tpad, _ = dq.shape
    tb = chunk * n_sub
    gla_steps = tlen // tb
    paged_steps = page_table.shape[1] // n_pages
    n_steps = bsz * gla_steps
    assert n_steps == dbsz * paged_steps and s0.shape[0] in (1, bsz)
    qrows = 1 << (tnew - 1).bit_length()
    orows = V7X_SUBLANES
    assert tnew <= qrows <= orows <= tpad
    n_rows = 2 * N_HEADS * qrows
    st = (1, N_HEADS, GLA_DK, GLA_DV)
    seq = lambda w: pl.BlockSpec((1, tb, w), lambda s, pt: (s // gla_steps, s % gla_steps, 0))
    s0_map = (lambda s, pt: (s // gla_steps, 0, 0, 0)) if s0.shape[0] == bsz else (lambda s, pt: (0, 0, 0, 0))
    tok_spec = pl.BlockSpec((1, tpad, DIFF_W), lambda s, pt: (s // paged_steps, 0, 0))

    def page_spec(j):
        return pl.BlockSpec((page * N_HEADS, DIFF_DV),
                            lambda s, pt: (pt[s // paged_steps, (s % paged_steps) * n_pages + j], 0))

    grid_spec = pltpu.PrefetchScalarGridSpec(
        num_scalar_prefetch=1,
        grid=(n_steps,),
        in_specs=[pl.BlockSpec(memory_space=pltpu.SMEM), seq(GLA_QK_W), seq(GLA_QK_W), seq(GLA_QK_W),
                  seq(GLA_WIDTH), pl.BlockSpec(st, s0_map), tok_spec, tok_spec, tok_spec]
        + [page_spec(j) for j in range(n_pages)] * 2,
        out_specs=[seq(GLA_WIDTH), pl.BlockSpec(st, lambda s, pt: (s // gla_steps, 0, 0, 0)),
                   pl.BlockSpec((1, orows, DIFF_W), lambda s, pt: (s // paged_steps, 0, 0))],
        scratch_shapes=[pltpu.VMEM((GLA_QK_W, GLA_WIDTH), F32), pltpu.VMEM((n_rows, 1), F32),
                        pltpu.VMEM((n_rows, 1), F32), pltpu.VMEM((n_rows, 2 * DIFF_DV), F32)],
    )
    return pl.pallas_call(
        functools.partial(_gla_paged_body, n_pages=n_pages, qrows=qrows, tnew=tnew, chunk=chunk, n_sub=n_sub,
                          gla_steps=gla_steps, paged_steps=paged_steps),
        grid_spec=grid_spec,
        out_shape=[jax.ShapeDtypeStruct((bsz, tlen, GLA_WIDTH), F32),
                   jax.ShapeDtypeStruct((bsz, N_HEADS, GLA_DK, GLA_DV), F32),
                   jax.ShapeDtypeStruct((dbsz, orows, DIFF_W), F32)],
        compiler_params=_cparams(("arbitrary",)),
        name="gla_scan_and_paged_attn",
    )(page_table, lam, gq, gk, glog, gv, s0, dq, kn, vn, *([cache_k] * n_pages), *([cache_v] * n_pages))


N_MIX_INPUTS = 11


def _mix_body(*refs, n_blocks):
    ins, outs = refs[:N_MIX_INPUTS], refs[-4:]
    i = pl.program_id(0)

    @pl.when(i < n_blocks)
    def _():
        _mix_tokens(*ins, *outs)

    @pl.when(i >= n_blocks)
    def _():
        for r in outs:
            r[...] = jnp.zeros_like(r)


def _mix_tokens(og_ref, gg_ref, od_ref, x_ref, ggain_ref, dgain_ref, wo_ref, n2_ref, wrh_ref, wrl_ref, br_ref,
                h_ref, xn_ref, ei_ref, gt_ref):
    def head_norm(z, gain):
        parts = []
        for h in range(N_HEADS):
            seg = z[:, GLA_DV * h:GLA_DV * (h + 1)]
            parts.append(seg * lax.rsqrt(jnp.mean(seg * seg, axis=-1, keepdims=True) + EPS))
        return jnp.concatenate(parts, axis=-1) * gain

    gg = gg_ref[...]
    a = head_norm(og_ref[...], ggain_ref[...]) * (gg / (1.0 + jnp.exp(-gg)))
    d = head_norm(od_ref[...], dgain_ref[...]) * (1.0 - LAM_INIT)
    cat = jnp.concatenate([a, d], axis=-1).astype(BF16)
    hres = x_ref[...] + _dot(cat, wo_ref[...])
    h_ref[...] = hres

    xn = hres * lax.rsqrt(jnp.mean(hres * hres, axis=-1, keepdims=True) + EPS) * n2_ref[...]
    _store_row_tiles(xn_ref, xn)

    x_hi, x_lo = _split_bf16(xn)
    logits = _dot(x_hi, wrh_ref[...]) + _dot(x_lo, wrh_ref[...]) + _dot(x_hi, wrl_ref[...]) + br_ref[...]
    lane = lax.broadcasted_iota(I32, logits.shape, 1).astype(F32)
    far = jnp.float32(1e4)

    def rmax(z):
        return jnp.max(z, axis=-1, keepdims=True)

    def first_lane(hit):
        return jnp.min(jnp.where(hit, lane, far), axis=-1, keepdims=True)

    is_group = lane < N_GROUPS
    lg = jnp.where(is_group, logits, NEG_BIG)
    mg = rmax(lg)
    grp = first_lane(lg == mg)
    g_gate = 1.0 / jnp.sum(jnp.where(is_group, jnp.exp(lg - mg), 0.0), axis=-1, keepdims=True)
    lo = N_GROUPS + EXPERTS_PER_GROUP * grp
    in_grp = (lane >= lo) & (lane < lo + EXPERTS_PER_GROUP)
    le = jnp.where(in_grp, logits, NEG_BIG)
    v1 = rmax(le)
    i1 = first_lane(in_grp & (le == v1))
    rest = in_grp & (lane != i1)
    le2 = jnp.where(rest, logits, NEG_BIG)
    v2 = rmax(le2)
    i2 = first_lane(rest & (le2 == v2))
    e21 = jnp.exp(v2 - v1)
    p1 = 1.0 / (1.0 + e21)
    ei = jnp.where(lane == 0, i1 - N_GROUPS, jnp.where(lane == 1, i2 - N_GROUPS, 0.0))
    ei_ref[...] = ei.astype(I32)
    gt_ref[...] = jnp.where(lane == 0, g_gate * p1, jnp.where(lane == 1, g_gate * (e21 * p1), 0.0))


def _mix_out(o_gla, gg, o_diff, x2d, prep, tile, n_total, row_start=0, base=None):
    n = x2d.shape[0]
    tm = _row_tile(n, tile)
    assert row_start % tm == 0 and n_total % tm == 0
    off = row_start // tm
    n_blocks = n // tm
    n_steps = n_blocks if base is not None else n_total // tm - off
    row = lambda w: pl.BlockSpec((tm, w), lambda i: (jnp.minimum(i, n_blocks - 1), 0))
    out_row = lambda w: pl.BlockSpec((tm, w), lambda i: (i + off, 0))
    full = lambda a: pl.BlockSpec(a.shape, lambda i: (0,) * a.ndim)
    consts = (prep["gla_gain"], prep["diff_gain"], prep["w_out"], prep["norm2"], prep["w_route_hi"],
              prep["w_route_lo"], prep["b_route"])
    base = () if base is None else tuple(base)
    n_in = 4 + len(consts)
    assert n_in == N_MIX_INPUTS
    return pl.pallas_call(
        functools.partial(_mix_body, n_blocks=n_blocks),
        grid=(n_steps,),
        in_specs=[row(GLA_WIDTH), row(GLA_WIDTH), row(DIFF_W), row(D_MODEL)] + [full(c) for c in consts]
        + [pl.BlockSpec(memory_space=pl.ANY)] * len(base),
        out_specs=[out_row(D_MODEL), pl.BlockSpec((tm * ROW_TILE, V7X_LANES), lambda i: (i + off, 0)),
                   out_row(V7X_LANES), out_row(V7X_LANES)],
        out_shape=[jax.ShapeDtypeStruct((n_total, D_MODEL), F32),
                   jax.ShapeDtypeStruct((n_total * ROW_TILE, V7X_LANES), F32),
                   jax.ShapeDtypeStruct((n_total, V7X_LANES), I32), jax.ShapeDtypeStruct((n_total, V7X_LANES), F32)],
        input_output_aliases={n_in + j: j for j in range(len(base))},
        compiler_params=_cparams(("arbitrary",)),
        name="mix_out_route",
    )(o_gla, gg, o_diff, x2d, *consts, *base)


def _rank_body(ei_ref, rank_ref, cnt_ref, carry_ref):
    i = pl.program_id(0)

    @pl.when(i == 0)
    def _():
        carry_ref[...] = jnp.zeros_like(carry_ref)

    ei = ei_ref[...]
    tb = ei.shape[0]
    lane = lax.broadcasted_iota(I32, ei.shape, 1)
    e0 = ei[:, 0:1]
    e1 = ei[:, 1:2]
    oh0 = lane == e0
    oh1 = lane == e1
    cnt = oh0.astype(F32) + oh1.astype(F32)
    ri = lax.broadcasted_iota(I32, (tb, tb), 0)
    ci = lax.broadcasted_iota(I32, (tb, tb), 1)
    before = _dot((ci < ri).astype(BF16), cnt.astype(BF16)) + carry_ref[0:1, :]
    r0 = jnp.sum(jnp.where(oh0, before, 0.0), axis=-1, keepdims=True)
    r1 = jnp.sum(jnp.where(oh1, before, 0.0), axis=-1, keepdims=True)
    rank_ref[...] = jnp.where(lane == 0, r0, jnp.where(lane == 1, r1, 0.0)).astype(I32)
    total = carry_ref[0:1, :] + jnp.sum(cnt, axis=0, keepdims=True)
    carry_ref[...] = jnp.broadcast_to(total, carry_ref.shape)
    cnt_ref[...] = jnp.broadcast_to(total, cnt_ref.shape).astype(I32)


def _rank(ei):
    n = ei.shape[0]
    tb = _row_tile(n, 256)
    row = pl.BlockSpec((tb, V7X_LANES), lambda i: (i, 0))
    one = pl.BlockSpec((V7X_SUBLANES, V7X_LANES), lambda i: (0, 0))
    return pl.pallas_call(
        _rank_body,
        grid=(n // tb,),
        in_specs=[row],
        out_specs=[row, one],
        out_shape=[jax.ShapeDtypeStruct((n, V7X_LANES), I32), jax.ShapeDtypeStruct((V7X_SUBLANES, V7X_LANES), I32)],
        scratch_shapes=[pltpu.VMEM((V7X_SUBLANES, V7X_LANES), F32)],
        compiler_params=_cparams(("arbitrary",)),
        name="moe_rank",
    )(ei)


def _tile_rows(ref, row):
    return ref.at[pl.ds(pl.multiple_of(row * ROW_TILE, ROW_TILE), ROW_TILE)]


def _issue_loop(n, per_item):
    assert n % DMA_ISSUE_UNROLL == 0

    def group(g, c):
        for u in range(DMA_ISSUE_UNROLL):
            per_item(g * DMA_ISSUE_UNROLL + u, u)
        return c

    lax.fori_loop(0, n // DMA_ISSUE_UNROLL, group, 0)


def _dispatch_body(dest_ref, zfill_ref, x_ref, xb_ref, zero_ref, sem, zsem, *, tb):
    i = pl.program_id(0)

    def zero_copy(e):
        start = pl.multiple_of(zfill_ref[e] * ROW_TILE, ROW_TILE)
        return pltpu.make_async_copy(zero_ref, xb_ref.at[pl.ds(start, MOE_BLOCK * ROW_TILE)], zsem)

    @pl.when(i == 0)
    def _():
        zero_ref[...] = jnp.zeros_like(zero_ref)
        for e in range(zfill_ref.shape[0]):
            @pl.when(zfill_ref[e] >= 0)
            def _():
                zero_copy(e).start()
        for e in range(zfill_ref.shape[0]):
            @pl.when(zfill_ref[e] >= 0)
            def _():
                zero_copy(e).wait()

    def copy(t, k):
        return pltpu.make_async_copy(_tile_rows(x_ref, t), _tile_rows(xb_ref, dest_ref[0, 0, 2 * t + k]), sem)

    def start(t, u):
        copy(t, 0).start(priority=0)
        copy(t, 1).start(priority=1)

    def wait(t, u):
        copy(t, 0).wait()
        copy(t, 1).wait()

    _issue_loop(tb, start)
    _issue_loop(tb, wait)


def _dispatch(dest, zfill, xn_tiles, cap):
    n = xn_tiles.shape[0] // ROW_TILE
    tb = _row_tile(n, 512)
    return pl.pallas_call(
        functools.partial(_dispatch_body, tb=tb),
        grid=(n // tb,),
        in_specs=[pl.BlockSpec((1, 1, 2 * tb), lambda i: (i, 0, 0), memory_space=pltpu.SMEM),
                  pl.BlockSpec(memory_space=pltpu.SMEM),
                  pl.BlockSpec((tb * ROW_TILE, V7X_LANES), lambda i: (i, 0))],
        out_specs=pl.BlockSpec(memory_space=pl.ANY),
        out_shape=jax.ShapeDtypeStruct((cap * ROW_TILE, V7X_LANES), F32),
        scratch_shapes=[pltpu.VMEM((MOE_BLOCK * ROW_TILE, V7X_LANES), F32), pltpu.SemaphoreType.DMA(()),
                        pltpu.SemaphoreType.DMA(())],
        compiler_params=_cparams(("arbitrary",)),
        name="moe_dispatch",
    )(dest.reshape(n // tb, 1, 2 * tb), zfill, xn_tiles)


def _expert_body(be_ref, bv_ref, xb_ref, w1_ref, w3_ref, w2_ref, yb_ref, w1b, w3b, w2b):
    i = pl.program_id(0)
    prev = be_ref[jnp.maximum(i - 1, 0)]

    @pl.when((i == 0) | (be_ref[i] != prev))
    def _():
        w1b[...] = w1_ref[0].astype(BF16)
        w3b[...] = w3_ref[0].astype(BF16)
        w2b[...] = w2_ref[0].astype(BF16)

    @pl.when(bv_ref[i] > 0)
    def _():
        x = _load_row_tiles(xb_ref, MOE_BLOCK).astype(BF16)
        up = _dot(x, w1b[...])
        hid = (up / (1.0 + jnp.exp(-up))) * _dot(x, w3b[...])
        _store_row_tiles(yb_ref, _dot(hid.astype(BF16), w2b[...]))

    @pl.when(bv_ref[i] == 0)
    def _():
        yb_ref[...] = jnp.zeros_like(yb_ref)


def _experts(blk_expert, blk_valid, xb, w_up, w_gate, w_down):
    cap = xb.shape[0] // ROW_TILE
    n_blk = cap // MOE_BLOCK
    w13 = pl.BlockSpec((1, D_MODEL, D_EXPERT), lambda i, be, bv: (be[i], 0, 0))
    w2 = pl.BlockSpec((1, D_EXPERT, D_MODEL), lambda i, be, bv: (be[i], 0, 0))
    rows = pl.BlockSpec((MOE_BLOCK * ROW_TILE, V7X_LANES), lambda i, be, bv: (i, 0))
    grid_spec = pltpu.PrefetchScalarGridSpec(
        num_scalar_prefetch=2,
        grid=(n_blk,),
        in_specs=[rows, w13, w13, w2],
        out_specs=rows,
        scratch_shapes=[pltpu.VMEM((D_MODEL, D_EXPERT), BF16), pltpu.VMEM((D_MODEL, D_EXPERT), BF16),
                        pltpu.VMEM((D_EXPERT, D_MODEL), BF16)],
    )
    return pl.pallas_call(
        _expert_body,
        grid_spec=grid_spec,
        out_shape=jax.ShapeDtypeStruct((cap * ROW_TILE, V7X_LANES), F32),
        compiler_params=_cparams(("arbitrary",)),
        name="moe_experts",
    )(blk_expert, blk_valid, xb, w_up, w_gate, w_down)


def _combine_body(dest_ref, next_ref, h_ref, gt_ref, yb_ref, o_ref, buf, sem, *, tb):
    i = pl.program_id(0)
    slot = i & 1

    def copy(idx_ref, s, t, k):
        return pltpu.make_async_copy(_tile_rows(yb_ref, idx_ref[0, 0, 2 * t + k]), _tile_rows(buf.at[s, k], t),
                                     sem.at[s])

    def issue(idx_ref, s):
        def start(t, u):
            copy(idx_ref, s, t, 0).start(priority=0)
            copy(idx_ref, s, t, 1).start(priority=1)

        _issue_loop(tb, start)

    @pl.when(i == 0)
    def _():
        issue(dest_ref, 0)

    @pl.when(i + 1 < pl.num_programs(0))
    def _():
        issue(next_ref, 1 - slot)

    def wait(t, u):
        copy(dest_ref, slot, t, 0).wait()
        copy(dest_ref, slot, t, 1).wait()

    _issue_loop(tb, wait)
    gt = gt_ref[...]
    y0 = _load_row_tiles(buf.at[slot, 0], tb)
    y1 = _load_row_tiles(buf.at[slot, 1], tb)
    o_ref[...] = h_ref[...] + gt[:, 0:1] * y0 + gt[:, 1:2] * y1


def _combine(dest, hres, gt, yb, row_start, n_rows):
    n_total = hres.shape[0]
    tb = _row_tile(n_rows, 256)
    assert row_start % tb == 0 and n_total % tb == 0
    off = row_start // tb
    n_steps = n_rows // tb
    idx = lambda f: pl.BlockSpec((1, 1, 2 * tb), lambda i: (f(i) + off, 0, 0), memory_space=pltpu.SMEM)
    return pl.pallas_call(
        functools.partial(_combine_body, tb=tb),
        grid=(n_steps,),
        in_specs=[idx(lambda i: i), idx(lambda i: jnp.minimum(i + 1, n_steps - 1)),
                  pl.BlockSpec((tb, D_MODEL), lambda i: (i + off, 0)),
                  pl.BlockSpec((tb, V7X_LANES), lambda i: (i + off, 0)),
                  pl.BlockSpec(memory_space=pl.ANY)],
        out_specs=pl.BlockSpec((tb, D_MODEL), lambda i: (i, 0)),
        out_shape=jax.ShapeDtypeStruct((n_rows, D_MODEL), F32),
        scratch_shapes=[pltpu.VMEM((2, 2, tb * ROW_TILE, V7X_LANES), F32), pltpu.SemaphoreType.DMA((2,))],
        compiler_params=_cparams(("arbitrary",)),
        name="moe_combine",
    )(dest.reshape(n_total // tb, 1, 2 * tb), dest.reshape(n_total // tb, 1, 2 * tb), hres, gt, yb)


def _moe(hres, xn, ei, gt, w_up, w_gate, w_down, splits):
    n = hres.shape[0]
    rank, counts = _rank(ei)
    counts = counts[0, :N_EXPERTS]
    padded = (counts + MOE_BLOCK - 1) // MOE_BLOCK * MOE_BLOCK
    pad_end = jnp.cumsum(padded)
    pad_start = pad_end - padded
    e2 = ei[:, :2]
    start_of = jnp.sum(jnp.where(e2[..., None] == jnp.arange(N_EXPERTS, dtype=I32), pad_start, 0), axis=-1)
    dest = (start_of + rank[:, :2]).reshape(-1)
    n_blk = -(-(2 * n + N_EXPERTS * (MOE_BLOCK - 1)) // MOE_BLOCK)
    blk_start = jnp.arange(n_blk, dtype=I32) * MOE_BLOCK
    blk_expert = jnp.sum((pad_end[None, :] <= blk_start[:, None]).astype(I32), axis=1)
    blk_expert = jnp.minimum(blk_expert, N_EXPERTS - 1)
    blk_valid = (blk_start < (pad_start + counts)[blk_expert]).astype(I32)
    last_used = jnp.max(jnp.where(blk_valid > 0, blk_expert, 0))
    blk_expert = jnp.where(blk_start < pad_end[-1], blk_expert, last_used)
    tail_start = blk_start[n_blk - N_EXPERTS:]
    zfill = jnp.concatenate([jnp.where(padded > 0, pad_end - MOE_BLOCK, -1),
                             jnp.where(tail_start >= pad_end[-1], tail_start, -1)]).astype(I32)
    xb = _dispatch(dest, zfill, xn, n_blk * MOE_BLOCK)
    yb = _experts(blk_expert, blk_valid, xb, w_up, w_gate, w_down)
    return [_combine(dest, hres, gt, yb, start, rows) for start, rows in splits]


def _prepare(norm1, w_in, w_gk2, b_gk2, gla_norm, q_norm, k_norm, diff_norm, w_out, norm2,
             w_route_group, b_route_group, w_route_expert, b_route_expert):
    gq, gk, gv, gg, glr, dq, dk, dv = jnp.split(
        w_in, [256, 512, 1024, 1536, 1552, 2064, 2576], axis=-1)
    glr = jnp.pad(glr, ((0, 0), (0, V7X_LANES - GLA_RANK)))
    w_packed = jnp.concatenate([gq, gk, gv, gg, dq, dk, dv, glr], axis=-1).astype(BF16)
    w_route = jnp.concatenate([w_route_group, w_route_expert], axis=-1)
    w_route = jnp.pad(w_route, ((0, 0), (0, V7X_LANES - N_GROUPS - N_EXPERTS)))
    w_route_hi = w_route.astype(BF16)
    b_route = jnp.pad(jnp.concatenate([b_route_group, b_route_expert]), (0, V7X_LANES - N_GROUPS - N_EXPERTS))
    gidx = jnp.arange(256) // DIFF_DQK
    return {
        "norm1": norm1.reshape(1, D_MODEL),
        "w_in": w_packed,
        "w_gk2": jnp.pad(w_gk2, ((0, V7X_LANES - GLA_RANK), (0, 0))).astype(BF16),
        "b_gk2": b_gk2.reshape(1, GLA_QK_W),
        "q_gain": jnp.tile(q_norm.reshape(-1), N_HEADS).reshape(1, DIFF_W),
        "k_gain": jnp.tile(k_norm.reshape(-1), N_HEADS).reshape(1, DIFF_W),
        "group_ones": (gidx[:, None] == gidx[None, :]).astype(BF16),
        "gla_gain": jnp.tile(gla_norm, N_HEADS).reshape(1, GLA_WIDTH),
        "diff_gain": jnp.tile(diff_norm, N_HEADS).reshape(1, DIFF_W),
        "w_out": w_out.astype(BF16),
        "norm2": norm2.reshape(1, D_MODEL),
        "w_route_hi": w_route_hi,
        "w_route_lo": (w_route - w_route_hi.astype(F32)).astype(BF16),
        "b_route": b_route.reshape(1, V7X_LANES),
    }


def _fusion_plan(bsz, seq, dbsz, n_tbl):
    n_pages = PAGES_PER_STEP
    while n_pages >= 1:
        n_sub = GLA_SUBCHUNKS
        while n_sub >= 1 and n_tbl % n_pages == 0:
            tb = n_sub * GLA_CHUNK
            if seq % tb == 0 and bsz * (seq // tb) == dbsz * (n_tbl // n_pages):
                return n_sub, n_pages
            n_sub //= 2
        n_pages //= 2
    return None


def kernel(x_prompt, x_sample, cache_k, cache_v, state_gla, page_table, meta_tokens, norm1, w_in, w_gk2, b_gk2,
           gla_norm, q_norm, k_norm, lam_q1, lam_k1, lam_q2, lam_k2, diff_norm, w_out, norm2, w_route_group,
           b_route_group, w_route_expert, b_route_expert, w_up, w_gate, w_down):
    bsz, seq, _ = x_prompt.shape
    dbsz, dseq, _ = x_sample.shape
    n_phys, page = cache_k.shape[:2]
    prep = _prepare(norm1[0], w_in[0], w_gk2[0], b_gk2[0], gla_norm[0], q_norm[0], k_norm[0], diff_norm[0],
                    w_out[0], norm2[0], w_route_group[0], b_route_group[0], w_route_expert[0], b_route_expert[0])
    lam = (jnp.exp(jnp.sum(lam_q1[0] * lam_k1[0])) - jnp.exp(jnp.sum(lam_q2[0] * lam_k2[0])) + LAM_INIT).reshape(1)
    w_up, w_gate, w_down = w_up[0], w_gate[0], w_down[0]

    mq, mk, mv, _, mglog, _, mdkb, mdvb, mdk, mdv = _inproj(meta_tokens, prep, N_META)
    s_zero = jnp.zeros((1, N_HEADS, GLA_DK, GLA_DV), F32)
    _, s_meta = _gla(mq[None], mk[None], mglog[None], mv[None], s_zero, chunk=N_META, n_sub=1)

    xp = x_prompt.reshape(bsz * seq, D_MODEL)
    head_rows = lambda a: a.reshape(N_META * N_HEADS, DIFF_DV)
    gq, gk, gv, gg, glog, dq, dkb, dvb, k_rows, v_rows = _inproj(xp, prep, 512, seq=seq, meta_k=head_rows(mdk),
                                                                   meta_v=head_rows(mdv))
    seq3 = lambda a: a.reshape(bsz, seq, a.shape[-1])
    k_prompt = k_rows.reshape(bsz, seq + N_META, 1, N_HEADS, DIFF_DV)
    v_prompt = v_rows.reshape(bsz, seq + N_META, 1, N_HEADS, DIFF_DV)

    tpad = -(-dseq // V7X_BF16_SUBLANES) * V7X_BF16_SUBLANES
    xs = jnp.pad(x_sample, ((0, 0), (0, tpad - dseq), (0, 0))).reshape(dbsz * tpad, D_MODEL)
    sq, sk, sv, sg, sglog, sdq, sdkb, sdvb, sdk, sdv = _inproj(xs, prep, 128)
    pad3 = lambda a: a.reshape(dbsz, tpad, a.shape[-1])
    unpad = lambda a: pad3(a)[:, :dseq]
    rows = lambda a: a.reshape(dbsz * dseq, a.shape[-1])
    o_gla_s, gla_sample = _gla(pad3(sq), pad3(sk), pad3(sglog), pad3(sv), state_gla[:, 0], chunk=tpad, n_sub=1,
                               valid=dseq)

    cache_rows = lambda c: c.reshape(n_phys * page * N_HEADS, DIFF_DV)
    gla_args = (seq3(gq), seq3(gk), seq3(glog), seq3(gv), s_meta)
    paged_args = (page_table, lam, pad3(sdq).astype(F32), pad3(sdkb), pad3(sdvb), cache_rows(cache_k),
                  cache_rows(cache_v))
    plan = _fusion_plan(bsz, seq, dbsz, page_table.shape[1])
    if plan is not None:
        o_gla, gla_prompt, o_diff_s = _gla_and_paged(*gla_args, *paged_args, chunk=GLA_CHUNK, n_sub=plan[0],
                                                     n_pages=plan[1], page=page, tnew=dseq)
    else:
        n_sub = GLA_SUBCHUNKS if seq % (GLA_SUBCHUNKS * GLA_CHUNK) == 0 else 1
        o_gla, gla_prompt = _gla(*gla_args, chunk=GLA_CHUNK, n_sub=n_sub)
        o_diff_s = _attn_paged(*paged_args, page=page, tnew=dseq)

    bound = (8.1 * jnp.max(jnp.abs(q_norm[0])) * jnp.max(jnp.abs(k_norm[0]))).reshape(1)
    o_diff = _attn_prompt(lam, bound, seq3(dq), mdkb, mdvb, seq3(dkb), seq3(dvb), tq=_row_tile(seq, ATTN_TQ))
    n_p, n_s = bsz * seq, dbsz * dseq
    n_s_pad = -(-n_s // MOE_TOKEN_TILE) * MOE_TOKEN_TILE
    n_tok = n_p + n_s_pad
    tokens = _mix_out(o_gla.reshape(n_p, GLA_WIDTH), gg, o_diff.reshape(n_p, DIFF_W), xp, prep, MOE_TOKEN_TILE,
                      n_tok)

    tile_pad =lambda a: jnp.pad(rows(a), ((0, n_s_pad - n_s), (0, 0)))
    hres, xn_tiles, ei, gt = _mix_out(tile_pad(unpad(o_gla_s)), tile_pad(unpad(sg)), tile_pad(o_diff_s[:, :dseq]),
                                      tile_pad(x_sample), prep, MOE_TOKEN_TILE, n_tok, row_start=n_p, base=tokens)
    y_prompt, y_sample = _moe(hres, xn_tiles, ei, gt, w_up, w_gate, w_down, [(0, n_p), (n_p, n_s_pad)])
    y_prompt = y_prompt.reshape(bsz, seq, D_MODEL)
    y_sample = y_sample[:n_s].reshape(dbsz, dseq, D_MODEL)

    heads = lambda a: a.reshape(a.shape[0], a.shape[1], 1, N_HEADS, DIFF_DV)
    return (y_prompt, y_sample, k_prompt, v_prompt, gla_prompt[:, None],
            heads(unpad(sdk)), heads(unpad(sdv)), gla_sample[:, None])
```

```python
import functools

import jax
import jax.numpy as jnp
from jax import lax
from jax.experimental import pallas as pl
from jax.experimental.pallas import tpu as pltpu

F32 = jnp.float32
BF16 = jnp.bfloat16
I32 = jnp.int32

V7X_LANES = 128
V7X_SUBLANES = 8
V7X_BF16_SUBLANES = 16
V7X_VMEM_LIMIT_BYTES = 56 * 1024 * 1024

D_MODEL = 1024
N_META = 16
N_HEADS = 4
GLA_DK = 64
GLA_DV = 128
GLA_RANK = 16
GLA_TAU = 16.0
GLA_QK_W = N_HEADS * GLA_DK
GLA_WIDTH = N_HEADS * GLA_DV
DIFF_DQK = 64
DIFF_DV = 128
DIFF_W = N_HEADS * DIFF_DV
N_GROUPS = 4
EXPERTS_PER_GROUP = 8
N_EXPERTS = N_GROUPS * EXPERTS_PER_GROUP
D_EXPERT = D_MODEL // 2
EPS = 1e-6
LAM_INIT = 0.2
NEG_BIG = -1e30
EXP_CLAMP = 80.0

OFF_GQ, OFF_GK, OFF_GV, OFF_GG = 0, 256, 512, 1024
OFF_DQ, OFF_DK, OFF_DV, OFF_GLR = 1536, 2048, 2560, 3072
PACKED_IN_W = OFF_GLR + V7X_LANES

GLA_CHUNK = 64
GLA_SUBCHUNKS = 8
ATTN_TQ = 512
MOE_BLOCK = 256
MOE_TOKEN_TILE = 512
PAGES_PER_STEP = 32
PAGE_GROUP = 8
DMA_ISSUE_UNROLL = 8
SAFE_SCORE_BOUND = 40.0


def _cparams(sem):
    return pltpu.CompilerParams(dimension_semantics=sem, vmem_limit_bytes=V7X_VMEM_LIMIT_BYTES)


def _row_tile(n, pref):
    t = min(n, pref)
    while n % t:
        t //= 2
    return t


def _dot(a, b):
    return jnp.dot(a, b, preferred_element_type=F32)


def _dot_nt(a, b):
    return lax.dot_general(a, b, (((1,), (1,)), ((), ())), preferred_element_type=F32)


def _dot_tn(a, b):
    return lax.dot_general(a, b, (((0,), (0,)), ((), ())), preferred_element_type=F32)


def _iota_div(shape, dim, d):
    assert d & (d - 1) == 0
    return lax.shift_right_logical(lax.broadcasted_iota(I32, shape, dim), d.bit_length() - 1)


def _iota_mod(shape, dim, d):
    assert d & (d - 1) == 0
    return lax.broadcasted_iota(I32, shape, dim) & (d - 1)


ROW_TILE = D_MODEL // V7X_LANES


def _store_row_tiles(ref, x):
    rows = x.shape[0]
    for s in range(ROW_TILE):
        ref[pl.ds(s, rows, stride=ROW_TILE), :] = x[:, V7X_LANES * s:V7X_LANES * (s + 1)]


def _load_row_tiles(ref, rows):
    return jnp.concatenate([ref[pl.ds(s, rows, stride=ROW_TILE), :] for s in range(ROW_TILE)], axis=1)


def _split_bf16(x):
    hi = x.astype(BF16)
    lo = (x - hi.astype(F32)).astype(BF16)
    return hi, lo


def _inproj_body(*refs, seq_layout):
    x_ref, n1_ref, w_ref, wg2_ref, bg2_ref, qg_ref, kg_ref, gm_ref = refs[:8]
    if seq_layout is None:
        gq_ref, gk_ref, gv_ref, gg_ref, glog_ref, dq_ref, dkb_ref, dvb_ref, dk_ref, dv_ref = refs[8:]
    else:
        mk_ref, mv_ref = refs[8:10]
        gq_ref, gk_ref, gv_ref, gg_ref, glog_ref, dq_ref, dkb_ref, dvb_ref, kp_ref, vp_ref = refs[10:20]
        kbuf, vbuf, sem, msem = refs[20:]
    x = x_ref[...]
    ms = jnp.mean(x * x, axis=-1, keepdims=True)
    xn = (x * lax.rsqrt(ms + EPS) * n1_ref[...]).astype(BF16)

    def proj(off, width):
        return _dot(xn, w_ref[:, off:off + width])

    gq_ref[...] = proj(OFF_GQ, GLA_QK_W) * (GLA_DK ** -0.5)
    gk_ref[...] = proj(OFF_GK, GLA_QK_W)
    gv_ref[...] = proj(OFF_GV, GLA_WIDTH)
    gg_ref[...] = proj(OFF_GG, GLA_WIDTH)

    glr = proj(OFF_GLR, V7X_LANES).astype(BF16)
    gpre = _dot(glr, wg2_ref[...]) + bg2_ref[...]
    log_sig = jnp.minimum(gpre, 0.0) - jnp.log(1.0 + jnp.exp(-jnp.abs(gpre)))
    glog_ref[...] = log_sig * (1.0 / GLA_TAU)

    gm = gm_ref[...]

    def group_norm(z, gain):
        sq = (z * z).astype(BF16)
        ss = jnp.concatenate([_dot(sq[:, 256 * c:256 * (c + 1)], gm) for c in range(DIFF_W // 256)], axis=-1)
        return z * lax.rsqrt(ss * (1.0 / DIFF_DQK) + EPS) * gain

    dq = group_norm(proj(OFF_DQ, DIFF_W), qg_ref[...])
    dq_ref[...] = (dq * (DIFF_DQK ** -0.5)).astype(BF16)
    dk = group_norm(proj(OFF_DK, DIFF_W), kg_ref[...])
    dkb_ref[...] = dk.astype(BF16)
    dv = proj(OFF_DV, DIFF_W)
    dvb_ref[...] = dv.astype(BF16)
    if seq_layout is None:
        dk_ref[...] = dk
        dv_ref[...] = dv
        return

    seq, n_meta = seq_layout
    tm = x.shape[0]
    tiles_per_seq = seq // tm
    g = pl.program_id(0)
    slot = g & 1
    b = lax.shift_right_logical(g, tiles_per_seq.bit_length() - 1)
    seq_row0 = b * ((seq + n_meta) * N_HEADS)
    row0 = pl.multiple_of(seq_row0 + (n_meta + (g & (tiles_per_seq - 1)) * tm) * N_HEADS, V7X_SUBLANES)

    def copies(s):
        dst = pl.ds(row0, tm * N_HEADS)
        return (pltpu.make_async_copy(kbuf.at[s], kp_ref.at[dst], sem.at[s]),
                pltpu.make_async_copy(vbuf.at[s], vp_ref.at[dst], sem.at[s]))

    def meta_copies():
        dst = pl.ds(pl.multiple_of(seq_row0, V7X_SUBLANES), n_meta * N_HEADS)
        return (pltpu.make_async_copy(mk_ref, kp_ref.at[dst], msem),
                pltpu.make_async_copy(mv_ref, vp_ref.at[dst], msem))

    def wait_slot(s):
        for c in copies(s):
            c.wait()

    @pl.when(g >= 2)
    def _():
        wait_slot(slot)

    @pl.when((g & (tiles_per_seq - 1)) == 0)
    def _():
        for c in meta_copies():
            c.start()

    for h in range(N_HEADS):
        cols = slice(DIFF_DV * h, DIFF_DV * (h + 1))
        kbuf.at[slot][pl.ds(h, tm, stride=N_HEADS), :] = dk[:, cols]
        vbuf.at[slot][pl.ds(h, tm, stride=N_HEADS), :] = dv[:, cols]

    @pl.when((g & (tiles_per_seq - 1)) == 0)
    def _():
        for c in meta_copies():
            c.wait()

    for c in copies(slot):
        c.start()

    last = pl.num_programs(0) - 1

    @pl.when((g == last) & (g >= 1))
    def _():
        wait_slot(1 - slot)

    @pl.when(g == last)
    def _():
        wait_slot(slot)


def _inproj(x2d, prep, tile, seq=None, meta_k=None, meta_v=None):
    n = x2d.shape[0]
    tm = _row_tile(n, tile)
    row = lambda w: pl.BlockSpec((tm, w), lambda i: (i, 0))
    full = lambda a: pl.BlockSpec(a.shape, lambda i: (0,) * a.ndim)
    consts = (prep["norm1"], prep["w_in"], prep["w_gk2"], prep["b_gk2"], prep["q_gain"], prep["k_gain"],
              prep["group_ones"])
    widths = (GLA_QK_W, GLA_QK_W, GLA_WIDTH, GLA_WIDTH, GLA_QK_W, DIFF_W, DIFF_W, DIFF_W)
    dtypes = (F32, F32, F32, F32, F32, BF16, BF16, BF16)
    out_specs = [row(w) for w in widths]
    out_shape = [jax.ShapeDtypeStruct((n, w), d) for w, d in zip(widths, dtypes)]
    if seq is None:
        seq_layout, extra, scratch = None, (), []
        out_specs += [row(DIFF_W)] * 2
        out_shape += [jax.ShapeDtypeStruct((n, DIFF_W), F32)] * 2
    else:
        n_meta = meta_k.shape[0] // N_HEADS
        tiles_per_seq = seq // tm
        assert seq % tm == 0 and n % seq == 0 and tiles_per_seq & (tiles_per_seq - 1) == 0
        seq_layout, extra = (seq, n_meta), (meta_k, meta_v)
        final_rows = (n // seq) * (seq + n_meta) * N_HEADS
        out_specs += [pl.BlockSpec(memory_space=pl.ANY)] * 2
        out_shape += [jax.ShapeDtypeStruct((final_rows, DIFF_DV), F32)] * 2
        scratch = [pltpu.VMEM((2, tm * N_HEADS, DIFF_DV), F32), pltpu.VMEM((2, tm * N_HEADS, DIFF_DV), F32),
                   pltpu.SemaphoreType.DMA((2,)), pltpu.SemaphoreType.DMA(())]
    return pl.pallas_call(
        functools.partial(_inproj_body, seq_layout=seq_layout),
        grid=(n // tm,),
        in_specs=[row(D_MODEL)] + [full(c) for c in consts + extra],
        out_specs=out_specs,
        out_shape=out_shape,
        scratch_shapes=scratch,
        compiler_params=_cparams(("arbitrary",)),
        name="inproj",
    )(x2d, *consts, *extra)


def _gla_body(q_ref, k_ref, g_ref, v_ref, s0_ref, o_ref, sfin_ref, state_ref, *, chunk, n_sub, valid):
    t = pl.program_id(1)
    _gla_load_state(t == 0, s0_ref, state_ref)
    _interleave(_gla_chunks(q_ref, k_ref, g_ref, v_ref, o_ref, state_ref, chunk=chunk, n_sub=n_sub, valid=valid))
    _gla_store_state(t == pl.num_programs(1) - 1, sfin_ref, state_ref)


def _interleave(*stages):
    stages = list(stages)
    while stages:
        for gen in list(stages):
            if next(gen, StopIteration) is StopIteration:
                stages.remove(gen)


def _gla_load_state(first, s0_ref, state_ref):
    @pl.when(first)
    def _():
        state_ref[...] = jnp.zeros_like(state_ref)
        for h in range(N_HEADS):
            state_ref[GLA_DK * h:GLA_DK * (h + 1), GLA_DV * h:GLA_DV * (h + 1)] = s0_ref[0, h]


def _gla_store_state(last, sfin_ref, state_ref):
    @pl.when(last)
    def _():
        for h in range(N_HEADS):
            sfin_ref[0, h] = state_ref[GLA_DK * h:GLA_DK * (h + 1), GLA_DV * h:GLA_DV * (h + 1)]


def _gla_chunks(q_ref, k_ref, g_ref, v_ref, o_ref, state_ref, *, chunk, n_sub, valid):
    c = chunk
    ri = lax.broadcasted_iota(I32, (c, c), 0)
    ci = lax.broadcasted_iota(I32, (c, c), 1)
    tri = (ci <= ri).astype(BF16)
    ones_cols = jnp.ones((c, V7X_LANES), BF16)
    k_shape = (N_HEADS * c, GLA_QK_W)
    k_head_mask = _iota_div(k_shape, 0, c) == _iota_div(k_shape, 1, GLA_DK)
    v_shape = (N_HEADS * c, GLA_WIDTH)
    v_head_mask = _iota_div(v_shape, 0, c) == _iota_div(v_shape, 1, GLA_DV)
    s_shape = (GLA_QK_W, GLA_WIDTH)
    s_head_mask = _iota_div(s_shape, 0, GLA_DK) == _iota_div(s_shape, 1, GLA_DV)
    a_shape = (c, N_HEADS * c)
    causal = _iota_mod(a_shape, 1, c) <= lax.broadcasted_iota(I32, a_shape, 0)
    row_id = lax.broadcasted_iota(I32, (c, GLA_QK_W), 0)
    mid = c // 2 - 1

    state = state_ref[...]
    for sub in range(n_sub):
        rows = slice(sub * c, (sub + 1) * c)
        q = q_ref[0, rows, :]
        k = k_ref[0, rows, :]
        g = g_ref[0, rows, :]
        v = v_ref[0, rows, :]
        if valid is not None:
            g = jnp.where(row_id < valid, g, 0.0)
        g_hi, g_lo = _split_bf16(g)
        b = _dot(tri, g_hi) + _dot(tri, g_lo)
        b_last_col = _dot_tn(g_hi, ones_cols) + _dot_tn(g_lo, ones_cols)
        b_last = b[c - 1:c, :]
        b_mid = b[mid:mid + 1, :]

        q_dec = (q * jnp.exp(b)).astype(BF16)
        o_inter = _dot(q_dec, state.astype(BF16))

        q_t = (q * jnp.exp(jnp.minimum(b - b_mid, EXP_CLAMP))).astype(BF16)
        k_t = (k * jnp.exp(jnp.minimum(b_mid - b, EXP_CLAMP))).astype(BF16)
        k_rows = jnp.where(k_head_mask, jnp.concatenate([k_t] * N_HEADS, axis=0), 0)
        a = _dot_nt(q_t, k_rows)
        a = jnp.where(causal, a, 0.0).astype(BF16)
        v_bf = v.astype(BF16)
        v_rows = jnp.where(v_head_mask, jnp.concatenate([v_bf] * N_HEADS, axis=0), 0)
        o_ref[0, rows, :] = o_inter + _dot(a, v_rows)

        k_dec = (k * jnp.exp(b_last - b)).astype(BF16)
        ds = _dot_tn(k_dec, v_bf)
        decay = jnp.exp(jnp.concatenate([b_last_col] * (GLA_WIDTH // V7X_LANES), axis=1))
        state = decay * state + jnp.where(s_head_mask, ds, 0.0)
        yield
    state_ref[...] = state


def _gla(gq, gk, glog, gv, s0, *, chunk, n_sub, valid=None):
    bsz, tlen, _ = gq.shape
    tb = chunk * n_sub
    assert tlen % tb == 0
    s0_map = (lambda b, t: (b, 0, 0, 0)) if s0.shape[0] == bsz else (lambda b, t: (0, 0, 0, 0))
    seq = lambda w: pl.BlockSpec((1, tb, w), lambda b, t: (b, t, 0))
    st = (1, N_HEADS, GLA_DK, GLA_DV)
    return pl.pallas_call(
        functools.partial(_gla_body, chunk=chunk, n_sub=n_sub, valid=valid),
        grid=(bsz, tlen // tb),
        in_specs=[seq(GLA_QK_W), seq(GLA_QK_W), seq(GLA_QK_W), seq(GLA_WIDTH), pl.BlockSpec(st, s0_map)],
        out_specs=[seq(GLA_WIDTH), pl.BlockSpec(st, lambda b, t: (b, 0, 0, 0))],
        out_shape=[jax.ShapeDtypeStruct((bsz, tlen, GLA_WIDTH), F32),
                   jax.ShapeDtypeStruct((bsz, N_HEADS, GLA_DK, GLA_DV), F32)],
        scratch_shapes=[pltpu.VMEM((GLA_QK_W, GLA_WIDTH), F32)],
        compiler_params=_cparams(("arbitrary", "arbitrary")),
        name="gla_scan",
    )(gq, gk, glog, gv, s0)


def _attn_body(lam_ref, bound_ref, q_ref, km_ref, vm_ref, k_ref, v_ref, o_ref, l_ref, acc_ref, tri_ref, *, tq):
    i = pl.program_id(2)
    q = q_ref[0]
    lane = lax.broadcasted_iota(I32, (tq, DIFF_DV), 1)
    zero = jnp.zeros_like(q)
    qs = jnp.concatenate([jnp.where(lane < DIFF_DQK, q, zero), jnp.where(lane >= DIFF_DQK, q, zero)], axis=0)

    def diag_mask():
        return lax.broadcasted_iota(I32, (2 * tq, tq), 1) <= _iota_mod((2 * tq, tq), 0, tq)

    @pl.when((pl.program_id(0) == 0) & (pl.program_id(1) == 0) & (i == 0))
    def _():
        tri_ref[...] = diag_mask().astype(F32)

    bound = bound_ref[0]
    n_meta = km_ref.shape[0]

    def keys(j):
        return k_ref[0, pl.ds(pl.multiple_of(j * tq, tq), tq), :]

    def values(j):
        return v_ref[0, pl.ds(pl.multiple_of(j * tq, tq), tq), :]

    def finish(acc, l):
        o = acc / l
        o_ref[0] = o[:tq] - lam_ref[0] * o[tq:]

    @pl.when(bound <= SAFE_SCORE_BOUND)
    def _():
        def fold(p):
            return sum(p[:, V7X_LANES * c:V7X_LANES * (c + 1)] for c in range(tq // V7X_LANES))

        l_ref[...] = jnp.zeros_like(l_ref)
        acc_ref[...] = jnp.zeros_like(acc_ref)

        def full_chunk(j, carry):
            p = jnp.exp(_dot_nt(qs, keys(j)) - bound)
            l_ref[...] += fold(p)
            acc_ref[...] += _dot(p.astype(BF16), values(j))
            return carry

        lax.fori_loop(0, i, full_chunk, 0)
        p = jnp.exp(_dot_nt(qs, keys(i)) - bound) * tri_ref[...]
        pm = jnp.exp(_dot_nt(qs, km_ref[...]) - bound)
        acc = acc_ref[...] + _dot(p.astype(BF16), values(i)) + _dot(pm.astype(BF16), vm_ref[...])
        l = jnp.sum(l_ref[...] + fold(p), axis=-1, keepdims=True) + jnp.sum(pm, axis=-1, keepdims=True)
        finish(acc, l)

    @pl.when(bound > SAFE_SCORE_BOUND)
    def _():
        def step(k, v, carry, mask):
            m, l, acc = carry
            s = _dot_nt(qs, k)
            if mask is not None:
                s = jnp.where(mask, s, NEG_BIG)
            m_new = jnp.maximum(m, jnp.max(s, axis=-1, keepdims=True))
            alpha = jnp.exp(m - m_new)
            p = jnp.exp(s - m_new)
            l = alpha * l + jnp.sum(p, axis=-1, keepdims=True)
            acc = alpha * acc + _dot(p.astype(BF16), v)
            return m_new, l, acc

        carry = (jnp.full((2 * tq, 1), NEG_BIG, F32), jnp.zeros((2 * tq, 1), F32),
                 jnp.zeros((2 * tq, DIFF_DV), F32))
        carry = step(km_ref[...], vm_ref[...], carry, None)
        carry = lax.fori_loop(0, i, lambda j, c: step(keys(j), values(j), c, None), carry)
        _, l, acc = step(keys(i), values(i), carry, diag_mask())
        finish(acc, l)


def _attn_prompt(lam, bound, dq, km, vm, kb, vb, *, tq):
    bsz, tlen, _ = dq.shape
    assert tlen % tq == 0 and tq % V7X_LANES == 0
    meta_spec = pl.BlockSpec((km.shape[0], DIFF_DV), lambda b, h, i: (0, h))
    kv_spec = pl.BlockSpec((1, tlen, DIFF_DV), lambda b, h, i: (b, 0, h))
    q_spec = pl.BlockSpec((1, tq, DIFF_DV), lambda b, h, i: (b, i, h))
    smem = pl.BlockSpec(memory_space=pltpu.SMEM)
    return pl.pallas_call(
        functools.partial(_attn_body, tq=tq),
        grid=(bsz, N_HEADS, tlen // tq),
        in_specs=[smem, smem, q_spec, meta_spec, meta_spec, kv_spec, kv_spec],
        out_specs=q_spec,
        out_shape=jax.ShapeDtypeStruct((bsz, tlen, DIFF_W), F32),
        scratch_shapes=[pltpu.VMEM((2 * tq, V7X_LANES), F32), pltpu.VMEM((2 * tq, DIFF_DV), F32),
                        pltpu.VMEM((2 * tq, tq), F32)],
        compiler_params=_cparams(("arbitrary", "arbitrary", "arbitrary")),
        name="diff_attn_prompt",
    )(lam, bound, dq, km, vm, kb, vb)


def _attn_paged_body(pt_ref, lam_ref, q_ref, kn_ref, vn_ref, *rest, n_pages, qrows, tnew):
    del pt_ref
    k_refs = rest[:n_pages]
    v_refs = rest[n_pages:2 * n_pages]
    o_ref, m_ref, l_ref, acc_ref = rest[2 * n_pages:]
    g = pl.program_id(1)
    stats = (m_ref, l_ref, acc_ref)
    _paged_new_tokens(g == 0, q_ref, kn_ref, vn_ref, stats, qrows=qrows, tnew=tnew)
    _interleave(_paged_pages(q_ref, k_refs, v_refs, stats, qrows=qrows))
    _paged_finish(g == pl.num_programs(1) - 1, lam_ref, o_ref, stats, qrows=qrows)


def _paged_queries(q_ref, qrows):
    q = q_ref[0]
    lane = lax.broadcasted_iota(I32, (qrows, 2 * DIFF_DV), 1)
    blocks = []
    for h in range(N_HEADS):
        pair = q[0:qrows, 2 * DIFF_DV * (h // 2):2 * DIFF_DV * (h // 2 + 1)]
        for c in range(2):
            lo = DIFF_DV * (h % 2) + DIFF_DQK * c
            blocks.append(jnp.where((lane >= lo) & (lane < lo + DIFF_DQK), pair, 0.0))
    return jnp.concatenate(blocks, axis=0).astype(BF16)


def _paged_new_tokens(first, q_ref, kn_ref, vn_ref, stats, *, qrows, tnew):
    m_ref, l_ref, acc_ref = stats
    n_rows = 2 * N_HEADS * qrows
    tpad = kn_ref.shape[1]

    @pl.when(first)
    def _():
        qr = _paged_queries(q_ref, qrows)
        kn = jnp.concatenate([kn_ref[0, :, 0:2 * DIFF_DV], kn_ref[0, :, 2 * DIFF_DV:]], axis=0)
        vn = jnp.concatenate([vn_ref[0, :, 0:2 * DIFF_DV], vn_ref[0, :, 2 * DIFF_DV:]], axis=0)
        shape = (n_rows, 2 * tpad)
        same_pair = _iota_div(shape, 0, 4 * qrows) == _iota_div(shape, 1, tpad)
        tok = _iota_mod(shape, 1, tpad)
        visible = same_pair & (tok <= _iota_mod(shape, 0, qrows)) & (tok < tnew)
        s = jnp.where(visible, _dot_nt(qr, kn), NEG_BIG)
        m = jnp.max(s, axis=-1, keepdims=True)
        p = jnp.exp(s - m)
        m_ref[...] = m
        l_ref[...] = jnp.sum(p, axis=-1, keepdims=True)
        acc_ref[...] = _dot(p.astype(BF16), vn)


def _paged_pages(q_ref, k_refs, v_refs, stats, *, qrows):
    m_ref, l_ref, acc_ref = stats
    n_rows = 2 * N_HEADS * qrows
    n_pages = len(k_refs)
    qr = _paged_queries(q_ref, qrows)

    def pack_pairs(ref):
        half = ref.shape[0] // 2
        return jnp.concatenate([ref[pl.ds(0, half, stride=2), :], ref[pl.ds(1, half, stride=2), :]],
                               axis=1).astype(BF16)

    tile = (n_rows, V7X_LANES)
    pair_bias = jnp.where(_iota_mod(tile, 1, 2) == _iota_div(tile, 0, 4 * qrows), 0.0, NEG_BIG)
    cols = k_refs[0].shape[0] // 2
    bias = jnp.concatenate([pair_bias] * (cols // V7X_LANES), axis=1)
    m, l, acc = m_ref[...], l_ref[...], acc_ref[...]
    for first in range(0, n_pages, PAGE_GROUP):
        group = range(first, min(first + PAGE_GROUP, n_pages))
        scores = [_dot_nt(qr, pack_pairs(k_refs[j])) + bias for j in group]
        m_new = m
        for s in scores:
            m_new = jnp.maximum(m_new, jnp.max(s, axis=-1, keepdims=True))
        alpha = jnp.exp(m - m_new)
        l, acc = alpha * l, alpha * acc
        for j, s in zip(group, scores):
            p = jnp.exp(s - m_new)
            l = l + jnp.sum(p, axis=-1, keepdims=True)
            acc = acc + _dot(p.astype(BF16), pack_pairs(v_refs[j]))
        m = m_new
        yield
    m_ref[...], l_ref[...], acc_ref[...] = m, l, acc


def _paged_finish(last, lam_ref, o_ref, stats, *, qrows):
    _, l_ref, acc_ref = stats

    @pl.when(last)
    def _():
        o = acc_ref[...] / l_ref[...]
        lam = lam_ref[0]
        pad = jnp.zeros((o_ref.shape[1] - qrows, DIFF_DV), F32)
        for h in range(N_HEADS):
            r0 = 2 * qrows * h
            half = slice(DIFF_DV * (h % 2), DIFF_DV * (h % 2 + 1))
            res = o[r0:r0 + qrows, half] - lam * o[r0 + qrows:r0 + 2 * qrows, half]
            o_ref[0, :, DIFF_DV * h:DIFF_DV * (h + 1)] = jnp.concatenate([res, pad], axis=0)


def _attn_paged(page_table, lam, dq, kn, vn, cache_k, cache_v, *, page, tnew):
    dbsz, tpad, _ = dq.shape
    n_tbl = page_table.shape[1]
    qrows = 1 << (tnew - 1).bit_length()
    orows = V7X_SUBLANES
    assert tnew <= qrows <= orows <= tpad
    n_pages = PAGES_PER_STEP
    while n_tbl % n_pages:
        n_pages //= 2
    n_rows = 2 * N_HEADS * qrows
    tok_spec = pl.BlockSpec((1, tpad, DIFF_W), lambda b, g, pt: (b, 0, 0))
    out_spec = pl.BlockSpec((1, orows, DIFF_W), lambda b, g, pt: (b, 0, 0))

    def page_spec(j):
        return pl.BlockSpec((page * N_HEADS, DIFF_DV), lambda b, g, pt: (pt[b, g * n_pages + j], 0))

    grid_spec = pltpu.PrefetchScalarGridSpec(
        num_scalar_prefetch=1,
        grid=(dbsz, n_tbl // n_pages),
        in_specs=[pl.BlockSpec(memory_space=pltpu.SMEM), tok_spec, tok_spec, tok_spec]
        + [page_spec(j) for j in range(n_pages)] * 2,
        out_specs=out_spec,
        scratch_shapes=[pltpu.VMEM((n_rows, 1), F32), pltpu.VMEM((n_rows, 1), F32),
                        pltpu.VMEM((n_rows, 2 * DIFF_DV), F32)],
    )
    return pl.pallas_call(
        functools.partial(_attn_paged_body, n_pages=n_pages, qrows=qrows, tnew=tnew),
        grid_spec=grid_spec,
        out_shape=jax.ShapeDtypeStruct((dbsz, orows, DIFF_W), F32),
        compiler_params=_cparams(("arbitrary", "arbitrary")),
        name="diff_attn_paged",
    )(page_table, lam, dq, kn, vn, *([cache_k] * n_pages), *([cache_v] * n_pages))


def _gla_paged_body(pt_ref, lam_ref, gq_ref, gk_ref, gg_ref, gv_ref, s0_ref, q_ref, kn_ref, vn_ref, ck_ref, cv_ref,
                    og_ref, sfin_ref, oa_ref, state_ref, m_ref, l_ref, acc_ref, kbuf, vbuf, sem, *,
                    n_pages, page_rows, qrows, tnew, chunk, n_sub, gla_steps, paged_steps):
    step = pl.program_id(0)
    t = _split_step(step, gla_steps)[1]
    g = _split_step(step, paged_steps)[1]
    slot = step & 1

    def page_copies(s, into):
        b, grp = _split_step(s, paged_steps)
        copies = []
        for j in range(n_pages):
            src = pl.ds(pl.multiple_of(pt_ref[b, grp * n_pages + j] * page_rows, page_rows), page_rows)
            dst = pl.ds(j * page_rows, page_rows)
            copies += [pltpu.make_async_copy(ck_ref.at[src], kbuf.at[into, dst], sem.at[into]),
                       pltpu.make_async_copy(cv_ref.at[src], vbuf.at[into, dst], sem.at[into])]
        return copies

    @pl.when(step == 0)
    def _():
        for c in page_copies(step, 0):
            c.start()

    @pl.when(step + 1 < pl.num_programs(0))
    def _():
        for c in page_copies(step + 1, 1 - slot):
            c.start()

    for c in page_copies(step, slot):
        c.wait()
    k_refs = [kbuf.at[slot, pl.ds(j * page_rows, page_rows)] for j in range(n_pages)]
    v_refs = [vbuf.at[slot, pl.ds(j * page_rows, page_rows)] for j in range(n_pages)]

    stats = (m_ref, l_ref, acc_ref)
    _gla_load_state(t == 0, s0_ref, state_ref)
    _paged_new_tokens(g == 0, q_ref, kn_ref, vn_ref, stats, qrows=qrows, tnew=tnew)
    _interleave(_gla_chunks(gq_ref, gk_ref, gg_ref, gv_ref, og_ref, state_ref, chunk=chunk, n_sub=n_sub, valid=None),
                _paged_pages(q_ref, k_refs, v_refs, stats, qrows=qrows))
    _gla_store_state(t == gla_steps - 1, sfin_ref, state_ref)
    _paged_finish(g == paged_steps - 1, lam_ref, oa_ref, stats, qrows=qrows)


def _split_step(s, inner):
    assert inner & (inner - 1) == 0
    return lax.shift_right_logical(s, jnp.int32(inner.bit_length() - 1)), s & (inner - 1)


def _gla_and_paged(gq, gk, glog, gv, s0, page_table, lam, dq, kn, vn, cache_k, cache_v, *, chunk, n_sub, n_pages,
                   page, tnew):
    bsz, tlen, _ = gq.shape
    dbsz, tpad, _ = dq.shape
    tb = chunk * n_sub
    gla_steps = tlen // tb
    paged_steps = page_table.shape[1] // n_pages
    n_steps = bsz * gla_steps
    assert n_steps == dbsz * paged_steps and s0.shape[0] in (1, bsz)
    qrows = 1 << (tnew - 1).bit_length()
    orows = V7X_SUBLANES
    assert tnew <= qrows <= orows <= tpad
    n_rows = 2 * N_HEADS * qrows
    st = (1, N_HEADS, GLA_DK, GLA_DV)
    gla_of = lambda s: _split_step(s, gla_steps)
    paged_of = lambda s: _split_step(s, paged_steps)
    seq = lambda w: pl.BlockSpec((1, tb, w), lambda s, pt: (*gla_of(s), 0))
    s0_map = (lambda s, pt: (gla_of(s)[0], 0, 0, 0)) if s0.shape[0] == bsz else (lambda s, pt: (0, 0, 0, 0))
    tok_spec = pl.BlockSpec((1, tpad, DIFF_W), lambda s, pt: (paged_of(s)[0], 0, 0))

    page_rows = page * N_HEADS
    any_spec = pl.BlockSpec(memory_space=pl.ANY)
    page_buf = pltpu.VMEM((2, n_pages * page_rows, DIFF_DV), F32)
    grid_spec = pltpu.PrefetchScalarGridSpec(
        num_scalar_prefetch=1,
        grid=(n_steps,),
        in_specs=[pl.BlockSpec(memory_space=pltpu.SMEM), seq(GLA_QK_W), seq(GLA_QK_W), seq(GLA_QK_W),
                  seq(GLA_WIDTH), pl.BlockSpec(st, s0_map), tok_spec, tok_spec, tok_spec, any_spec, any_spec],
        out_specs=[seq(GLA_WIDTH), pl.BlockSpec(st, lambda s, pt: (gla_of(s)[0], 0, 0, 0)),
                   pl.BlockSpec((1, orows, DIFF_W), lambda s, pt: (paged_of(s)[0], 0, 0))],
        scratch_shapes=[pltpu.VMEM((GLA_QK_W, GLA_WIDTH), F32), pltpu.VMEM((n_rows, 1), F32),
                        pltpu.VMEM((n_rows, 1), F32), pltpu.VMEM((n_rows, 2 * DIFF_DV), F32),
                        page_buf, page_buf, pltpu.SemaphoreType.DMA((2,))],
    )
    return pl.pallas_call(
        functools.partial(_gla_paged_body, n_pages=n_pages, page_rows=page_rows, qrows=qrows, tnew=tnew, chunk=chunk,
                          n_sub=n_sub, gla_steps=gla_steps, paged_steps=paged_steps),
        grid_spec=grid_spec,
        out_shape=[jax.ShapeDtypeStruct((bsz, tlen, GLA_WIDTH), F32),
                   jax.ShapeDtypeStruct((bsz, N_HEADS, GLA_DK, GLA_DV), F32),
                   jax.ShapeDtypeStruct((dbsz, orows, DIFF_W), F32)],
        compiler_params=_cparams(("arbitrary",)),
        name="gla_scan_and_paged_attn",
    )(page_table, lam, gq, gk, glog, gv, s0, dq, kn, vn, cache_k, cache_v)


N_MIX_INPUTS = 11


def _mix_body(*refs, n_blocks):
    ins, outs = refs[:N_MIX_INPUTS], refs[-4:]
    i = pl.program_id(0)

    @pl.when(i < n_blocks)
    def _():
        _mix_tokens(*ins, *outs)

    @pl.when(i >= n_blocks)
    def _():
        for r in outs:
            r[...] = jnp.zeros_like(r)


def _mix_tokens(og_ref, gg_ref, od_ref, x_ref, ggain_ref, dgain_ref, wo_ref, n2_ref, wrh_ref, wrl_ref, br_ref,
                h_ref, xn_ref, ei_ref, gt_ref):
    def head_norm(z, gain):
        parts = []
        for h in range(N_HEADS):
            seg = z[:, GLA_DV * h:GLA_DV * (h + 1)]
            parts.append(seg * lax.rsqrt(jnp.mean(seg * seg, axis=-1, keepdims=True) + EPS))
        return jnp.concatenate(parts, axis=-1) * gain

    gg = gg_ref[...]
    a = head_norm(og_ref[...], ggain_ref[...]) * (gg / (1.0 + jnp.exp(-gg)))
    d = head_norm(od_ref[...], dgain_ref[...]) * (1.0 - LAM_INIT)
    cat = jnp.concatenate([a, d], axis=-1).astype(BF16)
    hres = x_ref[...] + _dot(cat, wo_ref[...])
    h_ref[...] = hres

    xn = hres * lax.rsqrt(jnp.mean(hres * hres, axis=-1, keepdims=True) + EPS) * n2_ref[...]
    _store_row_tiles(xn_ref, xn)

    x_hi, x_lo = _split_bf16(xn)
    logits = _dot(x_hi, wrh_ref[...]) + _dot(x_lo, wrh_ref[...]) + _dot(x_hi, wrl_ref[...]) + br_ref[...]
    lane = lax.broadcasted_iota(I32, logits.shape, 1).astype(F32)
    far = jnp.float32(1e4)

    def rmax(z):
        return jnp.max(z, axis=-1, keepdims=True)

    def first_lane(hit):
        return jnp.min(jnp.where(hit, lane, far), axis=-1, keepdims=True)

    is_group = lane < N_GROUPS
    lg = jnp.where(is_group, logits, NEG_BIG)
    mg = rmax(lg)
    grp = first_lane(lg == mg)
    g_gate = 1.0 / jnp.sum(jnp.where(is_group, jnp.exp(lg - mg), 0.0), axis=-1, keepdims=True)
    lo = N_GROUPS + EXPERTS_PER_GROUP * grp
    in_grp = (lane >= lo) & (lane < lo + EXPERTS_PER_GROUP)
    le = jnp.where(in_grp, logits, NEG_BIG)
    v1 = rmax(le)
    i1 = first_lane(in_grp & (le == v1))
    rest = in_grp & (lane != i1)
    le2 = jnp.where(rest, logits, NEG_BIG)
    v2 = rmax(le2)
    i2 = first_lane(rest & (le2 == v2))
    e21 = jnp.exp(v2 - v1)
    p1 = 1.0 / (1.0 + e21)
    ei = jnp.where(lane == 0, i1 - N_GROUPS, jnp.where(lane == 1, i2 - N_GROUPS, 0.0))
    ei_ref[...] = ei.astype(I32)
    gt_ref[...] = jnp.where(lane == 0, g_gate * p1, jnp.where(lane == 1, g_gate * (e21 * p1), 0.0))


def _mix_out(o_gla, gg, o_diff, x2d, prep, tile, n_total, row_start=0, base=None):
    n = x2d.shape[0]
    tm = _row_tile(n, tile)
    assert row_start % tm == 0 and n_total % tm == 0
    off = row_start // tm
    n_blocks = n // tm
    n_steps = n_blocks if base is not None else n_total // tm - off
    row = lambda w: pl.BlockSpec((tm, w), lambda i: (jnp.minimum(i, n_blocks - 1), 0))
    out_row = lambda w: pl.BlockSpec((tm, w), lambda i: (i + off, 0))
    full = lambda a: pl.BlockSpec(a.shape, lambda i: (0,) * a.ndim)
    consts = (prep["gla_gain"], prep["diff_gain"], prep["w_out"], prep["norm2"], prep["w_route_hi"],
              prep["w_route_lo"], prep["b_route"])
    base = () if base is None else tuple(base)
    n_in = 4 + len(consts)
    assert n_in == N_MIX_INPUTS
    return pl.pallas_call(
        functools.partial(_mix_body, n_blocks=n_blocks),
        grid=(n_steps,),
        in_specs=[row(GLA_WIDTH), row(GLA_WIDTH), row(DIFF_W), row(D_MODEL)] + [full(c) for c in consts]
        + [pl.BlockSpec(memory_space=pl.ANY)] * len(base),
        out_specs=[out_row(D_MODEL), pl.BlockSpec((tm * ROW_TILE, V7X_LANES), lambda i: (i + off, 0)),
                   out_row(V7X_LANES), out_row(V7X_LANES)],
        out_shape=[jax.ShapeDtypeStruct((n_total, D_MODEL), F32),
                   jax.ShapeDtypeStruct((n_total * ROW_TILE, V7X_LANES), F32),
                   jax.ShapeDtypeStruct((n_total, V7X_LANES), I32), jax.ShapeDtypeStruct((n_total, V7X_LANES), F32)],
        input_output_aliases={n_in + j: j for j in range(len(base))},
        compiler_params=_cparams(("arbitrary",)),
        name="mix_out_route",
    )(o_gla, gg, o_diff, x2d, *consts, *base)


def _rank_body(ei_ref, rank_ref, cnt_ref, carry_ref):
    i = pl.program_id(0)

    @pl.when(i == 0)
    def _():
        carry_ref[...] = jnp.zeros_like(carry_ref)

    ei = ei_ref[...]
    tb = ei.shape[0]
    lane = lax.broadcasted_iota(I32, ei.shape, 1)
    e0 = ei[:, 0:1]
    e1 = ei[:, 1:2]
    oh0 = lane == e0
    oh1 = lane == e1
    cnt = oh0.astype(F32) + oh1.astype(F32)
    ri = lax.broadcasted_iota(I32, (tb, tb), 0)
    ci = lax.broadcasted_iota(I32, (tb, tb), 1)
    before = _dot((ci < ri).astype(BF16), cnt.astype(BF16)) + carry_ref[0:1, :]
    r0 = jnp.sum(jnp.where(oh0, before, 0.0), axis=-1, keepdims=True)
    r1 = jnp.sum(jnp.where(oh1, before, 0.0), axis=-1, keepdims=True)
    rank_ref[...] = jnp.where(lane == 0, r0, jnp.where(lane == 1, r1, 0.0)).astype(I32)
    total = carry_ref[0:1, :] + jnp.sum(cnt, axis=0, keepdims=True)
    carry_ref[...] = jnp.broadcast_to(total, carry_ref.shape)
    cnt_ref[...] = jnp.broadcast_to(total, cnt_ref.shape).astype(I32)


def _rank(ei):
    n = ei.shape[0]
    tb = _row_tile(n, 256)
    row = pl.BlockSpec((tb, V7X_LANES), lambda i: (i, 0))
    one = pl.BlockSpec((V7X_SUBLANES, V7X_LANES), lambda i: (0, 0))
    return pl.pallas_call(
        _rank_body,
        grid=(n // tb,),
        in_specs=[row],
        out_specs=[row, one],
        out_shape=[jax.ShapeDtypeStruct((n, V7X_LANES), I32), jax.ShapeDtypeStruct((V7X_SUBLANES, V7X_LANES), I32)],
        scratch_shapes=[pltpu.VMEM((V7X_SUBLANES, V7X_LANES), F32)],
        compiler_params=_cparams(("arbitrary",)),
        name="moe_rank",
    )(ei)


def _tile_rows(ref, row):
    return ref.at[pl.ds(pl.multiple_of(row * ROW_TILE, ROW_TILE), ROW_TILE)]


def _issue_loop(n, per_item):
    assert n % DMA_ISSUE_UNROLL == 0

    def group(g, c):
        for u in range(DMA_ISSUE_UNROLL):
            per_item(g * DMA_ISSUE_UNROLL + u, u)
        return c

    lax.fori_loop(0, n // DMA_ISSUE_UNROLL, group, 0)


def _dispatch_body(dest_ref, zfill_ref, x_ref, xb_ref, zero_ref, sem, zsem, *, tb):
    i = pl.program_id(0)

    def zero_copy(e):
        start = pl.multiple_of(zfill_ref[e] * ROW_TILE, ROW_TILE)
        return pltpu.make_async_copy(zero_ref, xb_ref.at[pl.ds(start, MOE_BLOCK * ROW_TILE)], zsem)

    @pl.when(i == 0)
    def _():
        zero_ref[...] = jnp.zeros_like(zero_ref)
        for e in range(zfill_ref.shape[0]):
            @pl.when(zfill_ref[e] >= 0)
            def _():
                zero_copy(e).start()
        for e in range(zfill_ref.shape[0]):
            @pl.when(zfill_ref[e] >= 0)
            def _():
                zero_copy(e).wait()

    def copy(t, k):
        return pltpu.make_async_copy(_tile_rows(x_ref, t), _tile_rows(xb_ref, dest_ref[0, 0, 2 * t + k]), sem)

    def start(t, u):
        copy(t, 0).start(priority=0)
        copy(t, 1).start(priority=1)

    def wait(t, u):
        copy(t, 0).wait()
        copy(t, 1).wait()

    _issue_loop(tb, start)
    _issue_loop(tb, wait)


def _dispatch(dest, zfill, xn_tiles, cap):
    n = xn_tiles.shape[0] // ROW_TILE
    tb = _row_tile(n, 512)
    return pl.pallas_call(
        functools.partial(_dispatch_body, tb=tb),
        grid=(n // tb,),
        in_specs=[pl.BlockSpec((1, 1, 2 * tb), lambda i: (i, 0, 0), memory_space=pltpu.SMEM),
                  pl.BlockSpec(memory_space=pltpu.SMEM),
                  pl.BlockSpec((tb * ROW_TILE, V7X_LANES), lambda i: (i, 0))],
        out_specs=pl.BlockSpec(memory_space=pl.ANY),
        out_shape=jax.ShapeDtypeStruct((cap * ROW_TILE, V7X_LANES), F32),
        scratch_shapes=[pltpu.VMEM((MOE_BLOCK * ROW_TILE, V7X_LANES), F32), pltpu.SemaphoreType.DMA(()),
                        pltpu.SemaphoreType.DMA(())],
        compiler_params=_cparams(("arbitrary",)),
        name="moe_dispatch",
    )(dest.reshape(n // tb, 1, 2 * tb), zfill, xn_tiles)


def _expert_body(be_ref, bv_ref, xb_ref, w1_ref, w3_ref, w2_ref, yb_ref, w1b, w3b, w2b):
    i = pl.program_id(0)
    prev = be_ref[jnp.maximum(i - 1, 0)]

    @pl.when((i == 0) | (be_ref[i] != prev))
    def _():
        w1b[...] = w1_ref[0].astype(BF16)
        w3b[...] = w3_ref[0].astype(BF16)
        w2b[...] = w2_ref[0].astype(BF16)

    @pl.when(bv_ref[i] > 0)
    def _():
        x = _load_row_tiles(xb_ref, MOE_BLOCK).astype(BF16)
        up = _dot(x, w1b[...])
        hid = (up / (1.0 + jnp.exp(-up))) * _dot(x, w3b[...])
        _store_row_tiles(yb_ref, _dot(hid.astype(BF16), w2b[...]))

    @pl.when(bv_ref[i] == 0)
    def _():
        yb_ref[...] = jnp.zeros_like(yb_ref)


def _experts(blk_expert, blk_valid, xb, w_up, w_gate, w_down):
    cap = xb.shape[0] // ROW_TILE
    n_blk = cap // MOE_BLOCK
    w13 = pl.BlockSpec((1, D_MODEL, D_EXPERT), lambda i, be, bv: (be[i], 0, 0))
    w2 = pl.BlockSpec((1, D_EXPERT, D_MODEL), lambda i, be, bv: (be[i], 0, 0))
    rows = pl.BlockSpec((MOE_BLOCK * ROW_TILE, V7X_LANES), lambda i, be, bv: (i, 0))
    grid_spec = pltpu.PrefetchScalarGridSpec(
        num_scalar_prefetch=2,
        grid=(n_blk,),
        in_specs=[rows, w13, w13, w2],
        out_specs=rows,
        scratch_shapes=[pltpu.VMEM((D_MODEL, D_EXPERT), BF16), pltpu.VMEM((D_MODEL, D_EXPERT), BF16),
                        pltpu.VMEM((D_EXPERT, D_MODEL), BF16)],
    )
    return pl.pallas_call(
        _expert_body,
        grid_spec=grid_spec,
        out_shape=jax.ShapeDtypeStruct((cap * ROW_TILE, V7X_LANES), F32),
        compiler_params=_cparams(("arbitrary",)),
        name="moe_experts",
    )(blk_expert, blk_valid, xb, w_up, w_gate, w_down)


def _combine_body(dest_ref, next_ref, h_ref, gt_ref, yb_ref, o_ref, buf, sem, *, tb):
    i = pl.program_id(0)
    slot = i & 1

    def copy(idx_ref, s, t, k):
        return pltpu.make_async_copy(_tile_rows(yb_ref, idx_ref[0, 0, 2 * t + k]), _tile_rows(buf.at[s, k], t),
                                     sem.at[s])

    def issue(idx_ref, s):
        def start(t, u):
            copy(idx_ref, s, t, 0).start(priority=0)
            copy(idx_ref, s, t, 1).start(priority=1)

        _issue_loop(tb, start)

    @pl.when(i == 0)
    def _():
        issue(dest_ref, 0)

    @pl.when(i + 1 < pl.num_programs(0))
    def _():
        issue(next_ref, 1 - slot)

    def wait(t, u):
        copy(dest_ref, slot, t, 0).wait()
        copy(dest_ref, slot, t, 1).wait()

    _issue_loop(tb, wait)
    gt = gt_ref[...]
    y0 = _load_row_tiles(buf.at[slot, 0], tb)
    y1 = _load_row_tiles(buf.at[slot, 1], tb)
    o_ref[...] = h_ref[...] + gt[:, 0:1] * y0 + gt[:, 1:2] * y1


def _combine(dest, hres, gt, yb, row_start, n_rows):
    n_total = hres.shape[0]
    tb = _row_tile(n_rows, 256)
    assert row_start % tb == 0 and n_total % tb == 0
    off = row_start // tb
    n_steps = n_rows // tb
    idx = lambda f: pl.BlockSpec((1, 1, 2 * tb), lambda i: (f(i) + off, 0, 0), memory_space=pltpu.SMEM)
    return pl.pallas_call(
        functools.partial(_combine_body, tb=tb),
        grid=(n_steps,),
        in_specs=[idx(lambda i: i), idx(lambda i: jnp.minimum(i + 1, n_steps - 1)),
                  pl.BlockSpec((tb, D_MODEL), lambda i: (i + off, 0)),
                  pl.BlockSpec((tb, V7X_LANES), lambda i: (i + off, 0)),
                  pl.BlockSpec(memory_space=pl.ANY)],
        out_specs=pl.BlockSpec((tb, D_MODEL), lambda i: (i, 0)),
        out_shape=jax.ShapeDtypeStruct((n_rows, D_MODEL), F32),
        scratch_shapes=[pltpu.VMEM((2, 2, tb * ROW_TILE, V7X_LANES), F32), pltpu.SemaphoreType.DMA((2,))],
        compiler_params=_cparams(("arbitrary",)),
        name="moe_combine",
    )(dest.reshape(n_total // tb, 1, 2 * tb), dest.reshape(n_total // tb, 1, 2 * tb), hres, gt, yb)


def _moe(hres, xn, ei, gt, w_up, w_gate, w_down, splits):
    n = hres.shape[0]
    rank, counts = _rank(ei)
    counts = counts[0, :N_EXPERTS]
    padded = (counts + MOE_BLOCK - 1) // MOE_BLOCK * MOE_BLOCK
    pad_end = jnp.cumsum(padded)
    pad_start = pad_end - padded
    e2 = ei[:, :2]
    start_of = jnp.sum(jnp.where(e2[..., None] == jnp.arange(N_EXPERTS, dtype=I32), pad_start, 0), axis=-1)
    dest = (start_of + rank[:, :2]).reshape(-1)
    n_blk = -(-(2 * n + N_EXPERTS * (MOE_BLOCK - 1)) // MOE_BLOCK)
    blk_start = jnp.arange(n_blk, dtype=I32) * MOE_BLOCK
    blk_expert = jnp.sum((pad_end[None, :] <= blk_start[:, None]).astype(I32), axis=1)
    blk_expert = jnp.minimum(blk_expert, N_EXPERTS - 1)
    blk_valid = (blk_start < (pad_start + counts)[blk_expert]).astype(I32)
    last_used = jnp.max(jnp.where(blk_valid > 0, blk_expert, 0))
    blk_expert = jnp.where(blk_start < pad_end[-1], blk_expert, last_used)
    tail_start = blk_start[n_blk - N_EXPERTS:]
    zfill = jnp.concatenate([jnp.where(padded > 0, pad_end - MOE_BLOCK, -1),
                             jnp.where(tail_start >= pad_end[-1], tail_start, -1)]).astype(I32)
    xb = _dispatch(dest, zfill, xn, n_blk * MOE_BLOCK)
    yb = _experts(blk_expert, blk_valid, xb, w_up, w_gate, w_down)
    return [_combine(dest, hres, gt, yb, start, rows) for start, rows in splits]


def _prepare(norm1, w_in, w_gk2, b_gk2, gla_norm, q_norm, k_norm, diff_norm, w_out, norm2,
             w_route_group, b_route_group, w_route_expert, b_route_expert):
    gq, gk, gv, gg, glr, dq, dk, dv = jnp.split(
        w_in, [256, 512, 1024, 1536, 1552, 2064, 2576], axis=-1)
    glr = jnp.pad(glr, ((0, 0), (0, V7X_LANES - GLA_RANK)))
    w_packed = jnp.concatenate([gq, gk, gv, gg, dq, dk, dv, glr], axis=-1).astype(BF16)
    w_route = jnp.concatenate([w_route_group, w_route_expert], axis=-1)
    w_route = jnp.pad(w_route, ((0, 0), (0, V7X_LANES - N_GROUPS - N_EXPERTS)))
    w_route_hi = w_route.astype(BF16)
    b_route = jnp.pad(jnp.concatenate([b_route_group, b_route_expert]), (0, V7X_LANES - N_GROUPS - N_EXPERTS))
    gidx = jnp.arange(256) // DIFF_DQK
    return {
        "norm1": norm1.reshape(1, D_MODEL),
        "w_in": w_packed,
        "w_gk2": jnp.pad(w_gk2, ((0, V7X_LANES - GLA_RANK), (0, 0))).astype(BF16),
        "b_gk2": b_gk2.reshape(1, GLA_QK_W),
        "q_gain": jnp.tile(q_norm.reshape(-1), N_HEADS).reshape(1, DIFF_W),
        "k_gain": jnp.tile(k_norm.reshape(-1), N_HEADS).reshape(1, DIFF_W),
        "group_ones": (gidx[:, None] == gidx[None, :]).astype(BF16),
        "gla_gain": jnp.tile(gla_norm, N_HEADS).reshape(1, GLA_WIDTH),
        "diff_gain": jnp.tile(diff_norm, N_HEADS).reshape(1, DIFF_W),
        "w_out": w_out.astype(BF16),
        "norm2": norm2.reshape(1, D_MODEL),
        "w_route_hi": w_route_hi,
        "w_route_lo": (w_route - w_route_hi.astype(F32)).astype(BF16),
        "b_route": b_route.reshape(1, V7X_LANES),
    }


def _fusion_plan(bsz, seq, dbsz, n_tbl):
    n_pages = PAGES_PER_STEP
    while n_pages >= 1:
        n_sub = GLA_SUBCHUNKS
        while n_sub >= 1 and n_tbl % n_pages == 0:
            tb = n_sub * GLA_CHUNK
            steps = (seq // tb, n_tbl // n_pages)
            if seq % tb == 0 and bsz * steps[0] == dbsz * steps[1] and all(n & (n - 1) == 0 for n in steps):
                return n_sub, n_pages
            n_sub //= 2
        n_pages //= 2
    return None


def kernel(x_prompt, x_sample, cache_k, cache_v, state_gla, page_table, meta_tokens, norm1, w_in, w_gk2, b_gk2,
           gla_norm, q_norm, k_norm, lam_q1, lam_k1, lam_q2, lam_k2, diff_norm, w_out, norm2, w_route_group,
           b_route_group, w_route_expert, b_route_expert, w_up, w_gate, w_down):
    bsz, seq, _ = x_prompt.shape
    dbsz, dseq, _ = x_sample.shape
    n_phys, page = cache_k.shape[:2]
    prep = _prepare(norm1[0], w_in[0], w_gk2[0], b_gk2[0], gla_norm[0], q_norm[0], k_norm[0], diff_norm[0],
                    w_out[0], norm2[0], w_route_group[0], b_route_group[0], w_route_expert[0], b_route_expert[0])
    lam = (jnp.exp(jnp.sum(lam_q1[0] * lam_k1[0])) - jnp.exp(jnp.sum(lam_q2[0] * lam_k2[0])) + LAM_INIT).reshape(1)
    w_up, w_gate, w_down = w_up[0], w_gate[0], w_down[0]

    mq, mk, mv, _, mglog, _, mdkb, mdvb, mdk, mdv = _inproj(meta_tokens, prep, N_META)
    s_zero = jnp.zeros((1, N_HEADS, GLA_DK, GLA_DV), F32)
    _, s_meta = _gla(mq[None], mk[None], mglog[None], mv[None], s_zero, chunk=N_META, n_sub=1)

    xp = x_prompt.reshape(bsz * seq, D_MODEL)
    head_rows = lambda a: a.reshape(N_META * N_HEADS, DIFF_DV)
    gq, gk, gv, gg, glog, dq, dkb, dvb, k_rows, v_rows = _inproj(xp, prep, 512, seq=seq, meta_k=head_rows(mdk),
                                                                   meta_v=head_rows(mdv))
    seq3 = lambda a: a.reshape(bsz, seq, a.shape[-1])
    k_prompt = k_rows.reshape(bsz, seq + N_META, 1, N_HEADS, DIFF_DV)
    v_prompt = v_rows.reshape(bsz, seq + N_META, 1, N_HEADS, DIFF_DV)

    tpad = -(-dseq // V7X_BF16_SUBLANES) * V7X_BF16_SUBLANES
    xs = jnp.pad(x_sample, ((0, 0), (0, tpad - dseq), (0, 0))).reshape(dbsz * tpad, D_MODEL)
    sq, sk, sv, sg, sglog, sdq, sdkb, sdvb, sdk, sdv = _inproj(xs, prep, 128)
    pad3 = lambda a: a.reshape(dbsz, tpad, a.shape[-1])
    unpad = lambda a: pad3(a)[:, :dseq]
    rows = lambda a: a.reshape(dbsz * dseq, a.shape[-1])
    o_gla_s, gla_sample = _gla(pad3(sq), pad3(sk), pad3(sglog), pad3(sv), state_gla[:, 0], chunk=tpad, n_sub=1,
                               valid=dseq)

    cache_rows = lambda c: c.reshape(n_phys * page * N_HEADS, DIFF_DV)
    gla_args = (seq3(gq), seq3(gk), seq3(glog), seq3(gv), s_meta)
    paged_args = (page_table, lam, pad3(sdq).astype(F32), pad3(sdkb), pad3(sdvb), cache_rows(cache_k),
                  cache_rows(cache_v))
    plan = _fusion_plan(bsz, seq, dbsz, page_table.shape[1])
    if plan is not None:
        o_gla, gla_prompt, o_diff_s = _gla_and_paged(*gla_args, *paged_args, chunk=GLA_CHUNK, n_sub=plan[0],
                                                     n_pages=plan[1], page=page, tnew=dseq)
    else:
        n_sub = GLA_SUBCHUNKS if seq % (GLA_SUBCHUNKS * GLA_CHUNK) == 0 else 1
        o_gla, gla_prompt = _gla(*gla_args, chunk=GLA_CHUNK, n_sub=n_sub)
        o_diff_s = _attn_paged(*paged_args, page=page, tnew=dseq)

    bound = (8.1 * jnp.max(jnp.abs(q_norm[0])) * jnp.max(jnp.abs(k_norm[0]))).reshape(1)
    o_diff = _attn_prompt(lam, bound, seq3(dq), mdkb, mdvb, seq3(dkb), seq3(dvb), tq=_row_tile(seq, ATTN_TQ))
    n_p, n_s = bsz * seq, dbsz * dseq
    n_s_pad = -(-n_s // MOE_TOKEN_TILE) * MOE_TOKEN_TILE
    n_tok = n_p + n_s_pad
    tokens = _mix_out(o_gla.reshape(n_p, GLA_WIDTH), gg, o_diff.reshape(n_p, DIFF_W), xp, prep, MOE_TOKEN_TILE,
                      n_tok)

    tile_pad =lambda a: jnp.pad(rows(a), ((0, n_s_pad - n_s), (0, 0)))
    hres, xn_tiles, ei, gt = _mix_out(tile_pad(unpad(o_gla_s)), tile_pad(unpad(sg)), tile_pad(o_diff_s[:, :dseq]),
                                      tile_pad(x_sample), prep, MOE_TOKEN_TILE, n_tok, row_start=n_p, base=tokens)
    y_prompt, y_sample = _moe(hres, xn_tiles, ei, gt, w_up, w_gate, w_down, [(0, n_p), (n_p, n_s_pad)])
    y_prompt = y_prompt.reshape(bsz, seq, D_MODEL)
    y_sample = y_sample[:n_s].reshape(dbsz, dseq, D_MODEL)

    heads = lambda a: a.reshape(a.shape[0], a.shape[1], 1, N_HEADS, DIFF_DV)
    return (y_prompt, y_sample, k_prompt, v_prompt, gla_prompt[:, None],
            heads(unpad(sdk)), heads(unpad(sdv)), gla_sample[:, None])
```
